```python
import math
import jax
import jax.numpy as jnp
from jax import lax
import numpy as np

D_MODEL = 1024
BATCH = 1
SEQ = 16384
DEPTH = 4

EPS = 1e-6
CONV_WIDTH = 4
CONV_PAD_LEFT = 2

SSD_HEAD_DIM = 64
SSD_INNER = D_MODEL
SSD_HEADS = SSD_INNER // SSD_HEAD_DIM
SSD_GROUPS = 2
SSD_STATE = 128
SSD_CHUNK = 256
SSD_CONV_CH = SSD_INNER + 2 * SSD_GROUPS * SSD_STATE
DT_MIN = 1e-3
DT_MAX = 1e-1

ATTN_HEAD_DIM = 64
ATTN_INNER = D_MODEL
ATTN_HEADS = ATTN_INNER // ATTN_HEAD_DIM
ATTN_KV_HEADS = ATTN_HEADS // 4
ATTN_KV = ATTN_KV_HEADS * ATTN_HEAD_DIM
WINDOW = 128
ATTN_BLOCK = 128
ROPE_THETA = 500000.0
ROPE_DIM = ATTN_HEAD_DIM // 4

LRU_WIDTH = 2 * D_MODEL
LRU_BLOCK = 256
LRU_BLOCKS = LRU_WIDTH // LRU_BLOCK
LRU_C = 8.0
LRU_A_MIN = 0.9
LRU_A_MAX = 0.999

EV_SIZES = (SSD_INNER, SSD_CONV_CH, 2 * SSD_HEADS, ATTN_INNER, ATTN_KV, ATTN_KV, ATTN_INNER)
EV_IN = SSD_INNER + SSD_CONV_CH + 2 * SSD_HEADS + 2 * ATTN_INNER + 2 * ATTN_KV
EV_OUT = SSD_INNER + ATTN_INNER
OD_IN = 2 * LRU_WIDTH

kernel_name = 'bidir_hybrid_ssd_swa_rglru'

F32 = jnp.float32


def rmsnorm(x, w):
    xf = x.astype(F32)
    y = xf * lax.rsqrt(jnp.mean(xf * xf, axis=-1, keepdims=True) + EPS)
    return (y * w.astype(F32)).astype(x.dtype)


def split_points(sizes):
    return [int(v) for v in np.cumsum(sizes)[:-1]]


def conv_centred(u, w, b):
    ch = u.shape[-1]
    y = lax.conv_general_dilated(
        u, w[:, None, :].astype(u.dtype), window_strides=(1,),
        padding=[(CONV_PAD_LEFT, CONV_WIDTH - 1 - CONV_PAD_LEFT)],
        dimension_numbers=('NWC', 'WIO', 'NWC'), feature_group_count=ch)
    return y + b.astype(u.dtype)


def segsum_exact(a):
    t = a.shape[-1]
    rep = jnp.broadcast_to(a[..., :, None], a.shape + (t,))
    rep = jnp.where(jnp.tril(jnp.ones((t, t), dtype=bool), -1), rep, 0.0)
    ss = jnp.cumsum(rep, axis=-2)
    return jnp.where(jnp.tril(jnp.ones((t, t), dtype=bool), 0), ss, -jnp.inf)


def ssd_chunked(x, dt, A, Bm, Cm):
    b, l, h, p = x.shape
    g, n = Bm.shape[2], Bm.shape[3]
    e = h // g
    pad = (-l) % SSD_CHUNK
    if pad:
        x = jnp.pad(x, ((0, 0), (0, pad), (0, 0), (0, 0)))
        dt = jnp.pad(dt, ((0, 0), (0, pad), (0, 0)))
        Bm = jnp.pad(Bm, ((0, 0), (0, pad), (0, 0), (0, 0)))
        Cm = jnp.pad(Cm, ((0, 0), (0, pad), (0, 0), (0, 0)))
    lp = l + pad
    nc = lp // SSD_CHUNK
    t = SSD_CHUNK
    xs = (x * dt[..., None]).reshape(b, nc, t, g, e, p)
    a = jnp.moveaxis((dt * A).reshape(b, nc, t, g, e), 2, -1)
    Bc = Bm.reshape(b, nc, t, g, n)
    Cc = Cm.reshape(b, nc, t, g, n)
    a_cs = jnp.cumsum(a, axis=-1)
    lower = jnp.tril(jnp.ones((t, t), dtype=bool))
    seg = a_cs[..., :, None] - a_cs[..., None, :]
    decay_in = jnp.exp(jnp.where(lower, seg, -jnp.inf))
    cb = jnp.einsum('bclgn,bcsgn->bcgls', Cc, Bc)
    y_diag = jnp.einsum('bcgls,bcgels,bcsgep->bclgep', cb, decay_in, xs)
    decay_to_end = jnp.exp(a_cs[..., -1:] - a_cs)
    states = jnp.einsum('bclgn,bcgel,bclgep->bcgepn', Bc, decay_to_end, xs)
    states = jnp.concatenate([jnp.zeros_like(states[:, :1]), states], axis=1)
    chunk_tot = jnp.pad(jnp.moveaxis(a_cs[..., -1], 1, -1), ((0, 0), (0, 0), (0, 0), (1, 0)))
    decay_chunk = jnp.exp(segsum_exact(chunk_tot))
    states = jnp.einsum('bgezc,bcgepn->bzgepn', decay_chunk, states)[:, :-1]
    y_off = jnp.einsum('bclgn,bcgepn,bcgel->bclgep', Cc, states, jnp.exp(a_cs))
    return (y_diag + y_off).reshape(b, lp, h, p)[:, :l]


def ssd_branch(z, xbc, dt_raw, conv_w, conv_b, dt_bias, a_log, d_skip, norm_w):
    b, l, _ = z.shape
    gn = SSD_GROUPS * SSD_STATE
    xbc = jax.nn.silu(conv_centred(xbc, conv_w, conv_b)).astype(F32)
    xh = xbc[..., :SSD_INNER].reshape(b, l, SSD_HEADS, SSD_HEAD_DIM)
    Bm = xbc[..., SSD_INNER:SSD_INNER + gn].reshape(b, l, SSD_GROUPS, SSD_STATE)
    Cm = xbc[..., SSD_INNER + gn:].reshape(b, l, SSD_GROUPS, SSD_STATE)
    dt = jax.nn.softplus(dt_raw.astype(F32).reshape(b, l, 2, SSD_HEADS) + dt_bias.astype(F32))
    A = -jnp.exp(a_log.astype(F32))
    y_fwd = ssd_chunked(xh, dt[:, :, 0], A[0], Bm, Cm)
    y_bwd = ssd_chunked(xh[:, ::-1], dt[:, ::-1, 1], A[1], Bm[:, ::-1], Cm[:, ::-1])[:, ::-1]
    y = y_fwd + y_bwd + xh * d_skip.astype(F32)[:, None]
    y = y.reshape(b, l, SSD_INNER) * jax.nn.silu(z.astype(F32))
    yg = y.reshape(b, l, SSD_GROUPS, SSD_INNER // SSD_GROUPS)
    yg = yg * lax.rsqrt(jnp.mean(yg * yg, axis=-1, keepdims=True) + EPS)
    return yg.reshape(b, l, SSD_INNER) * norm_w.astype(F32)


def rope_tables(l):
    inv = ROPE_THETA ** (-jnp.arange(0, ROPE_DIM, 2, dtype=F32) / ROPE_DIM)
    ang = jnp.arange(l, dtype=F32)[:, None] * inv[None, :]
    return jnp.cos(ang), jnp.sin(ang)


def partial_rope(t, cos, sin):
    half = ROPE_DIM // 2
    tf = t[..., :ROPE_DIM].astype(F32)
    t1, t2 = tf[..., :half], tf[..., half:]
    c = cos[None, :, None, :]
    s = sin[None, :, None, :]
    rot = jnp.concatenate([t1 * c - t2 * s, t2 * c + t1 * s], axis=-1)
    return jnp.concatenate([rot.astype(t.dtype), t[..., ROPE_DIM:]], axis=-1)


def band_blocks(t, nb):
    b, l, hk, d = t.shape
    tp = jnp.pad(t, ((0, 0), (ATTN_BLOCK, ATTN_BLOCK), (0, 0), (0, 0)))
    tp = tp.reshape(b, nb + 2, ATTN_BLOCK, hk, d)
    return jnp.concatenate([tp[:, :-2], tp[:, 1:-1], tp[:, 2:]], axis=2)


def window_attention(q, k, v, sink):
    b, l, _, d = q.shape
    nb = l // ATTN_BLOCK
    grp = ATTN_HEADS // ATTN_KV_HEADS
    qb = q.reshape(b, nb, ATTN_BLOCK, ATTN_KV_HEADS, grp, d)
    kb = band_blocks(k, nb)
    vb = band_blocks(v, nb)
    s = jnp.einsum('bnqhgd,bnkhd->bnhgqk', qb, kb, preferred_element_type=F32) * (d ** -0.5)
    qi = jnp.arange(ATTN_BLOCK)[:, None]
    kj = jnp.arange(3 * ATTN_BLOCK)[None, :]
    kpos = (jnp.arange(nb)[:, None, None] - 1) * ATTN_BLOCK + kj[None]
    valid = (jnp.abs(ATTN_BLOCK + qi - kj)[None] <= WINDOW) & (kpos >= 0) & (kpos < l)
    s = jnp.where(valid[None, :, None, None], s, -jnp.inf)
    sink_b = sink.astype(F32).reshape(ATTN_KV_HEADS, grp)[None, None, :, :, None]
    m = jnp.maximum(jnp.max(s, axis=-1), sink_b)
    p = jnp.exp(s - m[..., None])
    denom = jnp.sum(p, axis=-1) + jnp.exp(sink_b - m)
    probs = p / denom[..., None]
    o = jnp.einsum('bnhgqk,bnkhd->bnqhgd', probs, vb.astype(F32))
    return o.reshape(b, l, ATTN_HEADS * d)


def linear_combine(c1, c2):
    a1, b1 = c1
    a2, b2 = c2
    return a1 * a2, a2 * b1 + b2


def rg_lru_dir(xc, w_a, b_a, w_x, b_x, lam, reverse):
    b, l, _ = xc.shape
    xb = xc.reshape(b, l, LRU_BLOCKS, LRU_BLOCK)
    r = jax.nn.sigmoid(jnp.einsum('blhi,hij->blhj', xb, w_a.astype(F32))
                       + b_a.astype(F32).reshape(LRU_BLOCKS, LRU_BLOCK)).reshape(b, l, LRU_WIDTH)
    i = jax.nn.sigmoid(jnp.einsum('blhi,hij->blhj', xb, w_x.astype(F32))
                       + b_x.astype(F32).reshape(LRU_BLOCKS, LRU_BLOCK)).reshape(b, l, LRU_WIDTH)
    log_a = -LRU_C * r * jax.nn.softplus(-lam.astype(F32))
    a = jnp.exp(log_a)
    u = jnp.sqrt(-jnp.expm1(2.0 * log_a)) * (i * xc)
    if reverse:
        a, u = a[:, ::-1], u[:, ::-1]
    _, h = lax.associative_scan(linear_combine, (a, u), axis=1)
    return h[:, ::-1] if reverse else h


def even_layer(x, g_norm, w_in, conv_w, conv_b, dt_bias, a_log, d_skip, ssd_norm_w, sink, w_out, cos, sin):
    b, l, _ = x.shape
    h = rmsnorm(x, g_norm)
    u = h @ w_in
    z, xbc, dt_raw, q, k, v, gate = jnp.split(u, split_points(EV_SIZES), axis=-1)
    y_ssd = ssd_branch(z, xbc, dt_raw, conv_w, conv_b, dt_bias, a_log, d_skip, ssd_norm_w)
    q = partial_rope(q.reshape(b, l, ATTN_HEADS, ATTN_HEAD_DIM), cos, sin)
    k = partial_rope(k.reshape(b, l, ATTN_KV_HEADS, ATTN_HEAD_DIM), cos, sin)
    v = v.reshape(b, l, ATTN_KV_HEADS, ATTN_HEAD_DIM)
    y_att = window_attention(q, k, v, sink) * jax.nn.silu(gate.astype(F32))
    y = jnp.concatenate([y_ssd, y_att], axis=-1).astype(x.dtype)
    return x + y @ w_out


def odd_layer(x, g_norm, w_in, conv_w, conv_b, w_a, b_a, w_x, b_x, lam, w_out):
    h = rmsnorm(x, g_norm)
    u = h @ w_in
    xr, gate = u[..., :LRU_WIDTH], u[..., LRU_WIDTH:]
    xc = conv_centred(xr, conv_w, conv_b).astype(F32)
    y = (rg_lru_dir(xc, w_a[0], b_a[0], w_x[0], b_x[0], lam[0], False)
         + rg_lru_dir(xc, w_a[1], b_a[1], w_x[1], b_x[1], lam[1], True))
    y = y * jax.nn.silu(gate.astype(F32))
    return x + y.astype(x.dtype) @ w_out


def setup_inputs(seed: int = 0) -> dict:
    key = jax.random.key(seed)
    ks = jax.random.split(key, 24)
    n_ev = (DEPTH + 1) // 2
    n_od = DEPTH // 2

    def nrm(k, shape, scale):
        return scale * jax.random.normal(k, shape, F32)

    x = nrm(ks[0], (BATCH, SEQ, D_MODEL), 1.0)
    norm_w = 1.0 + nrm(ks[1], (DEPTH, D_MODEL), 0.02)
    final_norm_w = 1.0 + nrm(ks[2], (D_MODEL,), 0.02)
    ev_w_in = nrm(ks[3], (n_ev, D_MODEL, EV_IN), D_MODEL ** -0.5)
    ev_conv_w = nrm(ks[4], (n_ev, CONV_WIDTH, SSD_CONV_CH), CONV_WIDTH ** -0.5)
    ev_conv_b = nrm(ks[5], (n_ev, SSD_CONV_CH), 0.01)
    dt0 = jnp.exp(jax.random.uniform(ks[6], (n_ev, 2, SSD_HEADS), F32, math.log(DT_MIN), math.log(DT_MAX)))
    ev_dt_bias = dt0 + jnp.log(-jnp.expm1(-dt0))
    ev_a_log = jnp.log(jax.random.uniform(ks[7], (n_ev, 2, SSD_HEADS), F32, 1.0, 16.0))
    ev_d_skip = 1.0 + nrm(ks[8], (n_ev, SSD_HEADS), 0.1)
    ev_ssd_norm_w = 1.0 + nrm(ks[9], (n_ev, SSD_INNER), 0.02)
    ev_sink = nrm(ks[10], (n_ev, ATTN_HEADS), 0.5)
    ev_w_out = nrm(ks[11], (n_ev, EV_OUT, D_MODEL), EV_OUT ** -0.5)
    od_w_in = nrm(ks[12], (n_od, D_MODEL, OD_IN), D_MODEL ** -0.5)
    od_conv_w = nrm(ks[13], (n_od, CONV_WIDTH, LRU_WIDTH), CONV_WIDTH ** -0.5)
    od_conv_b = nrm(ks[14], (n_od, LRU_WIDTH), 0.01)
    od_w_a = nrm(ks[15], (n_od, 2, LRU_BLOCKS, LRU_BLOCK, LRU_BLOCK), LRU_BLOCK ** -0.5)
    od_b_a = nrm(ks[16], (n_od, 2, LRU_WIDTH), 0.01)
    od_w_x = nrm(ks[17], (n_od, 2, LRU_BLOCKS, LRU_BLOCK, LRU_BLOCK), LRU_BLOCK ** -0.5)
    od_b_x = nrm(ks[18], (n_od, 2, LRU_WIDTH), 0.01)
    a_c = jax.random.uniform(ks[19], (n_od, 2, LRU_WIDTH), F32, LRU_A_MIN, LRU_A_MAX)
    s0 = a_c ** (1.0 / LRU_C)
    od_lambda = jnp.log(s0) - jnp.log1p(-s0)
    od_w_out = nrm(ks[20], (n_od, LRU_WIDTH, D_MODEL), LRU_WIDTH ** -0.5)
    return {'x': x, 'norm_w': norm_w, 'final_norm_w': final_norm_w,
            'ev_w_in': ev_w_in, 'ev_conv_w': ev_conv_w, 'ev_conv_b': ev_conv_b,
            'ev_dt_bias': ev_dt_bias, 'ev_a_log': ev_a_log, 'ev_d_skip': ev_d_skip,
            'ev_ssd_norm_w': ev_ssd_norm_w, 'ev_sink': ev_sink, 'ev_w_out': ev_w_out,
            'od_w_in': od_w_in, 'od_conv_w': od_conv_w, 'od_conv_b': od_conv_b,
            'od_w_a': od_w_a, 'od_b_a': od_b_a, 'od_w_x': od_w_x, 'od_b_x': od_b_x,
            'od_lambda': od_lambda, 'od_w_out': od_w_out}


def reference(x, norm_w, final_norm_w, ev_w_in, ev_conv_w, ev_conv_b, ev_dt_bias, ev_a_log,
              ev_d_skip, ev_ssd_norm_w, ev_sink, ev_w_out, od_w_in, od_conv_w, od_conv_b,
              od_w_a, od_b_a, od_w_x, od_b_x, od_lambda, od_w_out):
    cos, sin = rope_tables(x.shape[1])
    for layer in range(DEPTH):
        j = layer // 2
        if layer % 2 == 0:
            x = even_layer(x, norm_w[layer], ev_w_in[j], ev_conv_w[j], ev_conv_b[j], ev_dt_bias[j],
                           ev_a_log[j], ev_d_skip[j], ev_ssd_norm_w[j], ev_sink[j], ev_w_out[j], cos, sin)
        else:
            x = odd_layer(x, norm_w[layer], od_w_in[j], od_conv_w[j], od_conv_b[j], od_w_a[j],
                          od_b_a[j], od_w_x[j], od_b_x[j], od_lambda[j], od_w_out[j])
    return rmsnorm(x, final_norm_w)
```

```python
import functools

import jax
import jax.numpy as jnp
from jax import lax
from jax.experimental import pallas as pl
from jax.experimental.pallas import tpu as pltpu

F32 = jnp.float32
BF16 = jnp.bfloat16
HI = lax.Precision.HIGHEST

D_MODEL = 1024
EPS = 1e-6
LANES = 128
SUBLANES = 8
HALO = 16
CONV_W = 4
CONV_LEFT = 2

SSD_HEADS = 16
SSD_HEAD_DIM = 64
SSD_STATE = 128
SSD_GROUPS = 2
SSD_CHUNK = 256
SSD_INNER = 1024
SSD_CONV_CH = 1536

ATTN_HEADS = 16
ATTN_KV_HEADS = 4
ATTN_HEAD_DIM = 64
ATTN_BLOCK = 128
ROPE_THETA = 500000.0
ROPE_DIM = 16

LRU_WIDTH = 2048
LRU_BLOCK = 256
LRU_BLOCKS = 8
LRU_C = 8.0

NEG = -1e30
VMEM_LIMIT = 56 * 1024 * 1024

IN_TILE_EVEN = 256
IN_TILE_ODD = 256
ATTN_TILE = 512
LRU_TILE = 256
LRU_SUB = LRU_TILE // SUBLANES
LRU_PITCH = LRU_SUB + SUBLANES


def _dot(a, b):
    return jnp.dot(a, b, preferred_element_type=F32)


def _rms(x, g):
    ms = jnp.mean(x * x, axis=-1, keepdims=True)
    return x * lax.rsqrt(ms + EPS) * g


def _silu(x):
    return x * jax.nn.sigmoid(x)


def _params(n_axes=1):
    return pltpu.CompilerParams(dimension_semantics=("arbitrary",) * n_axes,
                                vmem_limit_bytes=VMEM_LIMIT)


def _const_spec(shape):
    nd = len(shape)
    return pl.BlockSpec(shape, lambda i: (0,) * nd, pipeline_mode=pl.Buffered(1))


def _row_spec(tm, width, rev_n=None):
    if rev_n is None:
        return pl.BlockSpec((tm, width), lambda i: (i, 0))
    return pl.BlockSpec((tm, width), lambda i: (rev_n - 1 - i, 0))


def _halo_specs(tm, n_rows):
    hb = tm // HALO
    nh = n_rows // HALO
    prev = pl.BlockSpec((HALO, D_MODEL), lambda i: (jnp.maximum(i * hb - 1, 0), 0))
    main = pl.BlockSpec((tm, D_MODEL), lambda i: (i, 0))
    nxt = pl.BlockSpec((HALO, D_MODEL), lambda i: (jnp.minimum((i + 1) * hb, nh - 1), 0))
    return [prev, main, nxt]


def _fill_normed(xp_ref, xm_ref, xn_ref, g_ref, h_scr, tm):
    i = pl.program_id(0)
    n = pl.num_programs(0)
    g = g_ref[...]
    hp = _rms(xp_ref[...], g)
    hn = _rms(xn_ref[...], g)
    h_scr[0:HALO, :] = jnp.where(i > 0, hp, 0.0).astype(BF16)
    h_scr[HALO:HALO + tm, :] = _rms(xm_ref[...], g).astype(BF16)
    h_scr[HALO + tm:HALO + tm + HALO, :] = jnp.where(i < n - 1, hn, 0.0).astype(BF16)


def _conv_cols(u_scr, cw_ref, cb_ref, tm, c0, c1):
    base = HALO - CONV_LEFT
    acc = cb_ref[:, c0:c1] + cw_ref[0:1, c0:c1] * u_scr[base:base + tm, c0:c1]
    for k in range(1, CONV_W):
        acc = acc + cw_ref[k:k + 1, c0:c1] * u_scr[base + k:base + k + tm, c0:c1]
    return acc


def _even_in_body(xp_ref, xm_ref, xn_ref, g_ref, wz_ref, wx_ref, wdt_ref, wq_ref, wk_ref, wv_ref,
                  wg_ref, cw_ref, cb_ref, dtb_ref, cos_ref, sa_ref, sb_ref,
                  z_ref, xc_ref, b_ref, c_ref, dt_ref, q_ref, kz_ref, vz_ref, sg_ref,
                  h_scr, u_scr, *, tm):
    _fill_normed(xp_ref, xm_ref, xn_ref, g_ref, h_scr, tm)
    hm = h_scr[HALO:HALO + tm, :]

    u_scr[...] = _dot(h_scr[...], wx_ref[...])
    cstep = 256
    for c0 in range(0, SSD_CONV_CH, cstep):
        y = _silu(_conv_cols(u_scr, cw_ref, cb_ref, tm, c0, c0 + cstep))
        if c0 < SSD_INNER:
            xc_ref[:, c0:c0 + cstep] = y
        elif c0 < SSD_INNER + 256:
            b_ref[...] = y
        else:
            c_ref[...] = y

    z_ref[...] = _dot(hm, wz_ref[...])
    dt_ref[...] = jax.nn.softplus(_dot(hm, wdt_ref[...]) + dtb_ref[...])
    sg_ref[...] = _silu(_dot(hm, wg_ref[...]))

    cos_t = cos_ref[...]
    sin_a = sa_ref[...]
    sin_b = sb_ref[...]

    def rope(t):
        return (t * cos_t + pltpu.roll(t, LANES - ROPE_DIM // 2, 1) * sin_a
                + pltpu.roll(t, ROPE_DIM // 2, 1) * sin_b)

    qf = _dot(hm, wq_ref[...]) * (ATTN_HEAD_DIM ** -0.5)
    for t in range(D_MODEL // LANES):
        q_ref[:, t * LANES:(t + 1) * LANES] = rope(qf[:, t * LANES:(t + 1) * LANES]).astype(BF16)

    low = lax.broadcasted_iota(jnp.int32, (tm, LANES), 1) < ATTN_HEAD_DIM
    kf = _dot(hm, wk_ref[...])
    vf = _dot(hm, wv_ref[...])
    for j in range(2):
        kt = rope(kf[:, j * LANES:(j + 1) * LANES])
        vt = vf[:, j * LANES:(j + 1) * LANES]
        for dst, a in ((kz_ref, kt), (vz_ref, vt)):
            r = pltpu.roll(a, ATTN_HEAD_DIM, 1)
            c = 4 * j * LANES
            dst[:, c:c + LANES] = jnp.where(low, a, 0.0).astype(BF16)
            dst[:, c + LANES:c + 2 * LANES] = jnp.where(low, 0.0, r).astype(BF16)
            dst[:, c + 2 * LANES:c + 3 * LANES] = jnp.where(low, r, 0.0).astype(BF16)
            dst[:, c + 3 * LANES:c + 4 * LANES] = jnp.where(low, 0.0, a).astype(BF16)


def _even_in(x, g, wz, wx, wdt, wq, wk, wv, wg, cw, cb, dtb, cos_t, sin_a, sin_b):
    n_rows = x.shape[0]
    tm = IN_TILE_EVEN
    rows = functools.partial(_row_spec, tm)
    in_specs = _halo_specs(tm, n_rows) + [
        _const_spec((1, D_MODEL)),
        _const_spec(wz.shape), _const_spec(wx.shape), _const_spec(wdt.shape), _const_spec(wq.shape),
        _const_spec(wk.shape), _const_spec(wv.shape), _const_spec(wg.shape),
        _const_spec(cw.shape), _const_spec(cb.shape), _const_spec(dtb.shape),
        rows(LANES), rows(LANES), rows(LANES),
    ]
    out_shape = (
        jax.ShapeDtypeStruct((n_rows, SSD_INNER), F32),
        jax.ShapeDtypeStruct((n_rows, SSD_INNER), F32),
        jax.ShapeDtypeStruct((n_rows, 256), F32),
        jax.ShapeDtypeStruct((n_rows, 256), F32),
        jax.ShapeDtypeStruct((n_rows, LANES), F32),
        jax.ShapeDtypeStruct((n_rows, D_MODEL), BF16),
        jax.ShapeDtypeStruct((n_rows, D_MODEL), BF16),
        jax.ShapeDtypeStruct((n_rows, D_MODEL), BF16),
        jax.ShapeDtypeStruct((n_rows, D_MODEL), F32),
    )
    out_specs = (rows(SSD_INNER), rows(SSD_INNER), rows(256), rows(256), rows(LANES),
                 rows(D_MODEL), rows(D_MODEL), rows(D_MODEL), rows(D_MODEL))
    return pl.pallas_call(
        functools.partial(_even_in_body, tm=tm),
        grid=(n_rows // tm,),
        in_specs=in_specs, out_specs=out_specs, out_shape=out_shape,
        scratch_shapes=[pltpu.VMEM((tm + 2 * HALO, D_MODEL), BF16),
                        pltpu.VMEM((tm + 2 * HALO, SSD_CONV_CH), F32)],
        compiler_params=_params(), name="even_in",
    )(x, x, x, g, wz, wx, wdt, wq, wk, wv, wg, cw, cb, dtb, cos_t, sin_a, sin_b)


def _ssd_body(*refs, reverse, combine):
    if combine:
        (xc_ref, b_ref, c_ref, dt_ref, alog_ref, e_ref, z_ref, yb_ref, dsk_ref, nw_ref,
         y_ref, state, ytmp) = refs
    else:
        xc_ref, b_ref, c_ref, dt_ref, alog_ref, e_ref, y_ref, state = refs
        ytmp = y_ref
    t_len = SSD_CHUNK
    off = SSD_HEADS if reverse else 0

    @pl.when(pl.program_id(0) == 0)
    def _():
        state[...] = jnp.zeros_like(state)

    row = lax.broadcasted_iota(jnp.int32, (t_len, t_len), 0)
    col = lax.broadcasted_iota(jnp.int32, (t_len, t_len), 1)
    mask = (col >= row) if reverse else (col <= row)
    tri = jnp.where(mask, 1.0, 0.0)

    lane = lax.broadcasted_iota(jnp.int32, (1, LANES), 1)
    in_dir = (lane >= off) & (lane < off + SSD_HEADS)
    a_rate = jnp.where(in_dir, -jnp.exp(alog_ref[...]), 0.0)

    dt = dt_ref[...]
    a = dt * a_rate
    cs = jnp.dot(tri, a, precision=HI, preferred_element_type=F32)
    expand = e_ref[...]
    cs_x = jnp.dot(cs, expand, precision=HI, preferred_element_type=F32)
    dt_x = jnp.dot(dt, expand, precision=HI, preferred_element_type=F32)
    cs_t = cs.T
    last = 0 if reverse else t_len - 1
    cs_last = cs_x[last:last + 1, :]

    xc = xc_ref[...]
    xs = xc * dt_x
    xw_b = (xs * jnp.exp(cs_last - cs_x)).astype(BF16)
    ecs = jnp.exp(cs_x)
    b_f = b_ref[...]
    c_b = c_ref[...].astype(BF16)
    b_b = b_f.astype(BF16)
    b_t = b_f.T.astype(BF16)
    low = lax.broadcasted_iota(jnp.int32, (t_len, LANES), 1) < SSD_HEAD_DIM

    gw = SSD_INNER // SSD_GROUPS
    for g in range(SSD_GROUPS):
        c_g = c_b[:, g * SSD_STATE:(g + 1) * SSD_STATE]
        b_g = b_b[:, g * SSD_STATE:(g + 1) * SSD_STATE]
        cb = lax.dot_general(c_g, b_g, (((1,), (1,)), ((), ())), preferred_element_type=F32)
        s_g = state[:, g * gw:(g + 1) * gw]
        y_off = _dot(c_g, s_g.astype(BF16)) * ecs[:, g * gw:(g + 1) * gw]
        for tt in range(gw // LANES):
            t = g * (gw // LANES) + tt
            xt = xs[:, t * LANES:(t + 1) * LANES]
            acc = y_off[:, tt * LANES:(tt + 1) * LANES]
            for half in range(2):
                ln = off + 2 * t + half
                seg = cs[:, ln:ln + 1] - cs_t[ln:ln + 1, :]
                decay = jnp.exp(jnp.where(mask, seg, NEG))
                m = (cb * decay).astype(BF16)
                xh = jnp.where(low, xt, 0.0) if half == 0 else jnp.where(low, 0.0, xt)
                acc = acc + _dot(m, xh.astype(BF16))
            ytmp[:, t * LANES:(t + 1) * LANES] = acc
        state[:, g * gw:(g + 1) * gw] = (
            jnp.exp(cs_last[:, g * gw:(g + 1) * gw]) * s_g
            + _dot(b_t[g * SSD_STATE:(g + 1) * SSD_STATE, :], xw_b[:, g * gw:(g + 1) * gw]))

    if combine:
        y = ytmp[...] + yb_ref[...] + xc * dsk_ref[...]
        y = y * _silu(z_ref[...])
        for g in range(SSD_GROUPS):
            yg = y[:, g * gw:(g + 1) * gw]
            ms = jnp.mean(yg * yg, axis=-1, keepdims=True)
            y_ref[:, g * gw:(g + 1) * gw] = (
                yg * lax.rsqrt(ms + EPS) * nw_ref[:, g * gw:(g + 1) * gw]).astype(BF16)


def _ssd(xc, bm, cm, dt, alog_row, expand, reverse, combine_args=None):
    n_rows = xc.shape[0]
    t_len = SSD_CHUNK
    nc = n_rows // t_len
    rows = functools.partial(_row_spec, t_len, rev_n=nc if reverse else None)
    combine = combine_args is not None
    in_specs = [rows(SSD_INNER), rows(256), rows(256), rows(LANES),
                _const_spec((1, LANES)), _const_spec((LANES, SSD_INNER))]
    args = [xc, bm, cm, dt, alog_row, expand]
    scratch = [pltpu.VMEM((SSD_STATE, SSD_INNER), F32)]
    if combine:
        z, y_bwd, dskip_x, norm_w = combine_args
        in_specs += [rows(SSD_INNER), rows(SSD_INNER),
                     _const_spec((1, SSD_INNER)), _const_spec((1, SSD_INNER))]
        args += [z, y_bwd, dskip_x, norm_w]
        scratch.append(pltpu.VMEM((t_len, SSD_INNER), F32))
        out_dtype = BF16
    else:
        out_dtype = F32
    return pl.pallas_call(
        functools.partial(_ssd_body, reverse=reverse, combine=combine),
        grid=(nc,),
        in_specs=in_specs, out_specs=rows(SSD_INNER),
        out_shape=jax.ShapeDtypeStruct((n_rows, SSD_INNER), out_dtype),
        scratch_shapes=scratch,
        compiler_params=_params(), name="ssd_fwd" if combine else "ssd_bwd",
    )(*args)


def _attn_body(sink_ref, q_ref, kp_ref, km_ref, kn_ref, vp_ref, vm_ref, vn_ref, sg_ref, ys_ref,
               x_ref, w1_ref, w2_ref, o_ref, k_scr, v_scr, y_scr, *, tq):
    i = pl.program_id(0)
    n = pl.num_programs(0)
    blk = ATTN_BLOCK
    nq = tq // blk
    for scr, p_ref, m_ref, n_ref in ((k_scr, kp_ref, km_ref, kn_ref), (v_scr, vp_ref, vm_ref, vn_ref)):
        scr[0:blk, :] = p_ref[...]
        scr[blk:blk + tq, :] = m_ref[...]
        scr[blk + tq:blk + tq + blk, :] = n_ref[...]

    qi = lax.broadcasted_iota(jnp.int32, (blk, 3 * blk), 0)
    kj = lax.broadcasted_iota(jnp.int32, (blk, 3 * blk), 1)
    rel = kj - qi

    def one_block(b, carry):
        r0 = pl.multiple_of(b * blk, blk)
        gb = i * nq + b
        lo = jnp.where(gb == 0, blk, 0)
        hi = jnp.where(gb == n * nq - 1, 2 * blk, 3 * blk)
        bias = jnp.where((rel >= 0) & (rel <= 2 * blk) & (kj >= lo) & (kj < hi), 0.0, NEG)
        for t in range(D_MODEL // LANES):
            qt = q_ref[pl.ds(r0, blk), t * LANES:(t + 1) * LANES]
            acc = jnp.zeros((blk, LANES), F32)
            for half in range(2):
                c = (2 * (t // 2) + half) * LANES
                kz = k_scr[pl.ds(r0, 3 * blk), c:c + LANES]
                s = lax.dot_general(qt, kz, (((1,), (1,)), ((), ())), preferred_element_type=F32)
                s = s + bias
                sink = sink_ref[2 * t + half]
                m = jnp.maximum(jnp.max(s, axis=-1, keepdims=True), sink)
                p = jnp.exp(s - m)
                den = jnp.sum(p, axis=-1, keepdims=True) + jnp.exp(sink - m)
                pn = (p / den).astype(BF16)
                acc = acc + _dot(pn, v_scr[pl.ds(r0, 3 * blk), c:c + LANES])
            y_scr[pl.ds(r0, blk), t * LANES:(t + 1) * LANES] = (
                acc * sg_ref[pl.ds(r0, blk), t * LANES:(t + 1) * LANES]).astype(BF16)
        return carry

    lax.fori_loop(0, nq, one_block, 0)
    o_ref[...] = x_ref[...] + _dot(ys_ref[...], w1_ref[...]) + _dot(y_scr[...], w2_ref[...])


def _attn_out(sink, q, kz, vz, sg, y_ssd, x, w1, w2):
    n_rows = x.shape[0]
    tq = ATTN_TILE
    blk = ATTN_BLOCK
    per = tq // blk
    nb = n_rows // blk
    rows = functools.partial(_row_spec, tq)
    prev = pl.BlockSpec((blk, D_MODEL), lambda i: (jnp.maximum(i * per - 1, 0), 0))
    nxt = pl.BlockSpec((blk, D_MODEL), lambda i: (jnp.minimum((i + 1) * per, nb - 1), 0))
    in_specs = [pl.BlockSpec(memory_space=pltpu.SMEM),
                rows(D_MODEL), prev, rows(D_MODEL), nxt, prev, rows(D_MODEL), nxt,
                rows(D_MODEL), rows(D_MODEL), rows(D_MODEL),
                _const_spec(w1.shape), _const_spec(w2.shape)]
    return pl.pallas_call(
        functools.partial(_attn_body, tq=tq),
        grid=(n_rows // tq,),
        in_specs=in_specs, out_specs=rows(D_MODEL),
        out_shape=jax.ShapeDtypeStruct((n_rows, D_MODEL), F32),
        scratch_shapes=[pltpu.VMEM((tq + 2 * blk, D_MODEL), BF16),
                        pltpu.VMEM((tq + 2 * blk, D_MODEL), BF16),
                        pltpu.VMEM((tq, D_MODEL), BF16)],
        compiler_params=_params(), name="attn_out",
    )(sink, q, kz, kz, kz, vz, vz, vz, sg, y_ssd, x, w1, w2)


def _odd_in_body(xp_ref, xm_ref, xn_ref, g_ref, wx_ref, wg_ref, cw_ref, cb_ref,
                 xc_ref, sg_ref, h_scr, u_scr, *, tm):
    _fill_normed(xp_ref, xm_ref, xn_ref, g_ref, h_scr, tm)
    u_scr[...] = _dot(h_scr[...], wx_ref[...])
    cstep = 256
    for c0 in range(0, LRU_WIDTH, cstep):
        xc_ref[:, c0:c0 + cstep] = _conv_cols(u_scr, cw_ref, cb_ref, tm, c0, c0 + cstep)
    sg_ref[...] = _silu(_dot(h_scr[HALO:HALO + tm, :], wg_ref[...]))


def _odd_in(x, g, wx, wg, cw, cb):
    n_rows = x.shape[0]
    tm = IN_TILE_ODD
    rows = functools.partial(_row_spec, tm)
    in_specs = _halo_specs(tm, n_rows) + [
        _const_spec((1, D_MODEL)), _const_spec(wx.shape), _const_spec(wg.shape),
        _const_spec(cw.shape), _const_spec(cb.shape)]
    return pl.pallas_call(
        functools.partial(_odd_in_body, tm=tm),
        grid=(n_rows // tm,),
        in_specs=in_specs, out_specs=(rows(LRU_WIDTH), rows(LRU_WIDTH)),
        out_shape=(jax.ShapeDtypeStruct((n_rows, LRU_WIDTH), F32),
                   jax.ShapeDtypeStruct((n_rows, LRU_WIDTH), F32)),
        scratch_shapes=[pltpu.VMEM((tm + 2 * HALO, D_MODEL), BF16),
                        pltpu.VMEM((tm + 2 * HALO, LRU_WIDTH), F32)],
        compiler_params=_params(), name="odd_in",
    )(x, x, x, g, wx, wg, cw, cb)


def _lru_body(*refs, reverse, combine, final):
    if combine:
        if final:
            (xc_ref, w_ref, ba_ref, bx_ref, lam_ref, hb_ref, sg_ref, x_ref, wo_ref, fw_ref,
             o_ref, a_scr, u_scr, carry, y_scr) = refs
        else:
            (xc_ref, w_ref, ba_ref, bx_ref, lam_ref, hb_ref, sg_ref, x_ref, wo_ref,
             o_ref, a_scr, u_scr, carry, y_scr) = refs
    else:
        xc_ref, w_ref, ba_ref, bx_ref, lam_ref, o_ref, a_scr, u_scr, carry = refs
    sub, pitch = LRU_SUB, LRU_PITCH
    n_grp = LRU_WIDTH // LANES

    @pl.when(pl.program_id(0) == 0)
    def _():
        carry[...] = jnp.zeros_like(carry)

    rate = -LRU_C * jax.nn.softplus(-lam_ref[...])
    for blk in range(LRU_BLOCKS):
        c0 = blk * LRU_BLOCK
        xb = xc_ref[:, c0:c0 + LRU_BLOCK]
        ri = _dot(xb.astype(BF16), w_ref[blk])
        r = jax.nn.sigmoid(ri[:, :LRU_BLOCK] + ba_ref[:, c0:c0 + LRU_BLOCK])
        ig = jax.nn.sigmoid(ri[:, LRU_BLOCK:] + bx_ref[:, c0:c0 + LRU_BLOCK])
        log_a = r * rate[:, c0:c0 + LRU_BLOCK]
        a = jnp.exp(log_a)
        th = jnp.tanh(log_a)
        one_minus_a2 = -2.0 * th / (1.0 - th)
        u = jnp.sqrt(one_minus_a2) * (ig * xb)
        for half in range(LRU_BLOCK // LANES):
            gi = blk * (LRU_BLOCK // LANES) + half
            for s in range(SUBLANES):
                a_scr[gi, s * pitch:s * pitch + sub, :] = a[s * sub:(s + 1) * sub, half * LANES:(half + 1) * LANES]
                u_scr[gi, s * pitch:s * pitch + sub, :] = u[s * sub:(s + 1) * sub, half * LANES:(half + 1) * LANES]

    steps = list(range(sub))
    if reverse:
        steps = steps[::-1]
    j0 = steps[0]
    h = u_scr[:, pl.ds(j0, SUBLANES, stride=pitch), :]
    p = a_scr[:, pl.ds(j0, SUBLANES, stride=pitch), :]
    for j in steps[1:]:
        aj = a_scr[:, pl.ds(j, SUBLANES, stride=pitch), :]
        uj = u_scr[:, pl.ds(j, SUBLANES, stride=pitch), :]
        h = aj * h + uj
        p = aj * p
        u_scr[:, pl.ds(j, SUBLANES, stride=pitch), :] = h
        a_scr[:, pl.ds(j, SUBLANES, stride=pitch), :] = p

    c = carry[...]
    order = list(range(SUBLANES))
    if reverse:
        order = order[::-1]
    carry_in = [None] * SUBLANES
    for s in order:
        carry_in[s] = c
        c = p[:, s:s + 1, :] * c + h[:, s:s + 1, :]
    carry[...] = c

    for s in range(SUBLANES):
        hs = (u_scr[:, s * pitch:s * pitch + sub, :]
              + a_scr[:, s * pitch:s * pitch + sub, :] * carry_in[s])
        for gi in range(n_grp):
            rs = slice(s * sub, (s + 1) * sub)
            cs_ = slice(gi * LANES, (gi + 1) * LANES)
            if combine:
                y_scr[rs, cs_] = ((hs[gi] + hb_ref[rs, cs_]) * sg_ref[rs, cs_]).astype(BF16)
            else:
                o_ref[rs, cs_] = hs[gi]

    if combine:
        out = x_ref[...] + _dot(y_scr[...], wo_ref[...])
        if final:
            out = _rms(out, fw_ref[...])
        o_ref[...] = out


def _lru(xc, w_gates, b_a, b_x, lam, reverse, combine_args=None, final_w=None):
    n_rows = xc.shape[0]
    tm = LRU_TILE
    n = n_rows // tm
    rows = functools.partial(_row_spec, tm, rev_n=n if reverse else None)
    combine = combine_args is not None
    final = final_w is not None
    in_specs = [rows(LRU_WIDTH), _const_spec(w_gates.shape), _const_spec((1, LRU_WIDTH)),
                _const_spec((1, LRU_WIDTH)), _const_spec((1, LRU_WIDTH))]
    args = [xc, w_gates, b_a, b_x, lam]
    n_grp = LRU_WIDTH // LANES
    scratch = [pltpu.VMEM((n_grp, SUBLANES * LRU_PITCH, LANES), F32),
               pltpu.VMEM((n_grp, SUBLANES * LRU_PITCH, LANES), F32),
               pltpu.VMEM((n_grp, 1, LANES), F32)]
    if combine:
        h_bwd, sg, x, w_out = combine_args
        in_specs += [rows(LRU_WIDTH), rows(LRU_WIDTH), rows(D_MODEL), _const_spec(w_out.shape)]
        args += [h_bwd, sg, x, w_out]
        if final:
            in_specs.append(_const_spec((1, D_MODEL)))
            args.append(final_w)
        scratch.append(pltpu.VMEM((tm, LRU_WIDTH), BF16))
        out_w = D_MODEL
    else:
        out_w = LRU_WIDTH
    return pl.pallas_call(
        functools.partial(_lru_body, reverse=reverse, combine=combine, final=final),
        grid=(n,),
        in_specs=in_specs, out_specs=rows(out_w),
        out_shape=jax.ShapeDtypeStruct((n_rows, out_w), F32),
        scratch_shapes=scratch,
        compiler_params=_params(), name="lru_fwd" if combine else "lru_bwd",
    )(*args)


def _rope_tables(n_rows):
    inv = ROPE_THETA ** (-jnp.arange(0, ROPE_DIM, 2, dtype=F32) / ROPE_DIM)
    ang = jnp.arange(n_rows, dtype=F32)[:, None] * inv[None, :]
    cos, sin = jnp.cos(ang), jnp.sin(ang)
    half = ROPE_DIM // 2
    ones = jnp.ones((n_rows, ATTN_HEAD_DIM - ROPE_DIM), F32)
    zeros = jnp.zeros((n_rows, ATTN_HEAD_DIM - ROPE_DIM), F32)
    zero_h = jnp.zeros((n_rows, half), F32)
    reps = LANES // ATTN_HEAD_DIM
    cos_t = jnp.tile(jnp.concatenate([cos, cos, ones], axis=1), (1, reps))
    sin_a = jnp.tile(jnp.concatenate([-sin, zero_h, zeros], axis=1), (1, reps))
    sin_b = jnp.tile(jnp.concatenate([zero_h, sin, zeros], axis=1), (1, reps))
    return cos_t, sin_a, sin_b


def _pad_lanes(v):
    return jnp.pad(v.astype(F32), (0, LANES - v.shape[0]))[None, :]


def _even_layer(x, g_norm, w_in, conv_w, conv_b, dt_bias, a_log, d_skip, ssd_norm_w, sink, w_out, rope):
    sizes = (SSD_INNER, SSD_CONV_CH, 2 * SSD_HEADS, D_MODEL, 256, 256, D_MODEL)
    cuts = [0]
    for s in sizes:
        cuts.append(cuts[-1] + s)
    wz, wx, wdt, wq, wk, wv, wg = (w_in[:, cuts[k]:cuts[k + 1]].astype(BF16) for k in range(7))
    wdt = jnp.pad(wdt, ((0, 0), (0, LANES - 2 * SSD_HEADS)))
    dtb = _pad_lanes(dt_bias.reshape(-1))
    z, xc, bm, cm, dt, q, kz, vz, sg = _even_in(
        x, g_norm[None, :], wz, wx, wdt, wq, wk, wv, wg, conv_w, conv_b[None, :], dtb, *rope)

    lane = jnp.arange(LANES)[:, None]
    head = (jnp.arange(SSD_INNER) // SSD_HEAD_DIM)[None, :]
    alog_row = _pad_lanes(a_log.reshape(-1))
    y_bwd = _ssd(xc, bm, cm, dt, alog_row, (lane == head + SSD_HEADS).astype(F32), reverse=True)
    dskip_x = jnp.repeat(d_skip.astype(F32), SSD_HEAD_DIM)[None, :]
    y_ssd = _ssd(xc, bm, cm, dt, alog_row, (lane == head).astype(F32), reverse=False,
                 combine_args=(z, y_bwd, dskip_x, ssd_norm_w[None, :]))

    w_out_b = w_out.astype(BF16)
    return _attn_out(sink.astype(F32), q, kz, vz, sg, y_ssd, x, w_out_b[:SSD_INNER], w_out_b[SSD_INNER:])


def _odd_layer(x, g_norm, w_in, conv_w, conv_b, w_a, b_a, w_x, b_x, lam, w_out, final_w):
    w_in_b = w_in.astype(BF16)
    xc, sg = _odd_in(x, g_norm[None, :], w_in_b[:, :LRU_WIDTH], w_in_b[:, LRU_WIDTH:],
                     conv_w, conv_b[None, :])
    w_gates = jnp.concatenate([w_a, w_x], axis=-1).astype(BF16)
    h_bwd = _lru(xc, w_gates[1], b_a[1][None, :], b_x[1][None, :], lam[1][None, :], reverse=True)
    return _lru(xc, w_gates[0], b_a[0][None, :], b_x[0][None, :], lam[0][None, :], reverse=False,
                combine_args=(h_bwd, sg, x, w_out.astype(BF16)),
                final_w=None if final_w is None else final_w[None, :])


def kernel(x, norm_w, final_norm_w, ev_w_in, ev_conv_w, ev_conv_b, ev_dt_bias, ev_a_log, ev_d_skip, ev_ssd_norm_w, ev_sink, ev_w_out, od_w_in, od_conv_w, od_conv_b, od_w_a, od_b_a, od_w_x, od_b_x, od_lambda, od_w_out):
    batch, n_rows, _ = x.shape
    depth = norm_w.shape[0]
    assert batch == 1 and depth % 2 == 0
    rope = _rope_tables(n_rows)
    h = x[0]
    for layer in range(depth):
        j = layer // 2
        if layer % 2 == 0:
            h = _even_layer(h, norm_w[layer], ev_w_in[j], ev_conv_w[j], ev_conv_b[j], ev_dt_bias[j],
                            ev_a_log[j], ev_d_skip[j], ev_ssd_norm_w[j], ev_sink[j], ev_w_out[j], rope)
        else:
            h = _odd_layer(h, norm_w[layer], od_w_in[j], od_conv_w[j], od_conv_b[j], od_w_a[j],
                           od_b_a[j], od_w_x[j], od_b_x[j], od_lambda[j], od_w_out[j],
                           final_norm_w if layer == depth - 1 else None)
    return h[None]
```

```python
import functools

import jax
import jax.numpy as jnp
from jax import lax
from jax.experimental import pallas as pl
from jax.experimental.pallas import tpu as pltpu

F32 = jnp.float32
BF16 = jnp.bfloat16

D_MODEL = 1024
EPS = 1e-6
LANES = 128
SUBLANES = 8
HALO = 16
CONV_W = 4
CONV_LEFT = 2

SSD_HEADS = 16
SSD_HEAD_DIM = 64
SSD_STATE = 128
SSD_GROUPS = 2
SSD_CHUNK = 256
SSD_INNER = 1024
SSD_CONV_CH = 1536

ATTN_HEADS = 16
ATTN_KV_HEADS = 4
ATTN_HEAD_DIM = 64
ATTN_BLOCK = 128
ROPE_THETA = 500000.0
ROPE_DIM = 16

LRU_WIDTH = 2048
LRU_BLOCK = 256
LRU_BLOCKS = 8
LRU_C = 8.0

NEG = -1e30
VMEM_LIMIT = 56 * 1024 * 1024

IN_TILE_EVEN = 256
IN_TILE_ODD = 256
ATTN_TILE = 512
LRU_TILE = 256
LRU_SUB = LRU_TILE // SUBLANES
LRU_PITCH = LRU_SUB + SUBLANES


def _dot(a, b):
    return jnp.dot(a, b, preferred_element_type=F32)


def _split3(v):
    hi = v.astype(BF16)
    r1 = v - hi.astype(F32)
    mid = r1.astype(BF16)
    lo = (r1 - mid.astype(F32)).astype(BF16)
    return hi, mid, lo


def _dot3(*ops, rhs_split=True):
    if rhs_split:
        m, parts = ops[0], ops[1:]
        return _dot(m, parts[0]) + _dot(m, parts[1]) + _dot(m, parts[2])
    parts, m = ops[:3], ops[3]
    return _dot(parts[0], m) + _dot(parts[1], m) + _dot(parts[2], m)


def _rms(x, g):
    ms = jnp.mean(x * x, axis=-1, keepdims=True)
    return x * lax.rsqrt(ms + EPS) * g


def _silu(x):
    return x * jax.nn.sigmoid(x)


def _params(n_axes=1):
    return pltpu.CompilerParams(dimension_semantics=("arbitrary",) * n_axes,
                                vmem_limit_bytes=VMEM_LIMIT)


def _const_spec(shape):
    nd = len(shape)
    return pl.BlockSpec(shape, lambda i: (0,) * nd, pipeline_mode=pl.Buffered(1))


def _row_spec(tm, width, rev_n=None):
    if rev_n is None:
        return pl.BlockSpec((tm, width), lambda i: (i, 0))
    return pl.BlockSpec((tm, width), lambda i: (rev_n - 1 - i, 0))


def _halo_specs(tm, n_rows):
    hb = tm // HALO
    nh = n_rows // HALO
    prev = pl.BlockSpec((HALO, D_MODEL), lambda i: (jnp.maximum(i * hb - 1, 0), 0))
    main = pl.BlockSpec((tm, D_MODEL), lambda i: (i, 0))
    nxt = pl.BlockSpec((HALO, D_MODEL), lambda i: (jnp.minimum((i + 1) * hb, nh - 1), 0))
    return [prev, main, nxt]


def _fill_normed(xp_ref, xm_ref, xn_ref, g_ref, h_scr, tm):
    i = pl.program_id(0)
    n = pl.num_programs(0)
    g = g_ref[...]
    hp = _rms(xp_ref[...], g)
    hn = _rms(xn_ref[...], g)
    h_scr[0:HALO, :] = jnp.where(i > 0, hp, 0.0).astype(BF16)
    h_scr[HALO:HALO + tm, :] = _rms(xm_ref[...], g).astype(BF16)
    h_scr[HALO + tm:HALO + tm + HALO, :] = jnp.where(i < n - 1, hn, 0.0).astype(BF16)


def _conv_cols(u_scr, cw_ref, cb_ref, tm, c0, c1):
    base = HALO - CONV_LEFT
    acc = cb_ref[:, c0:c1] + cw_ref[0:1, c0:c1] * u_scr[base:base + tm, c0:c1]
    for k in range(1, CONV_W):
        acc = acc + cw_ref[k:k + 1, c0:c1] * u_scr[base + k:base + k + tm, c0:c1]
    return acc


def _even_in_body(xp_ref, xm_ref, xn_ref, g_ref, wz_ref, wx_ref, wdt_ref, wq_ref, wk_ref, wv_ref,
                  wg_ref, cw_ref, cb_ref, dtb_ref, cos_ref, sa_ref, sb_ref,
                  z_ref, xc_ref, b_ref, c_ref, dt_ref, q_ref, kz_ref, vz_ref, sg_ref,
                  h_scr, u_scr, *, tm):
    _fill_normed(xp_ref, xm_ref, xn_ref, g_ref, h_scr, tm)
    hm = h_scr[HALO:HALO + tm, :]

    u_scr[...] = _dot(h_scr[...], wx_ref[...])
    cstep = 256
    for c0 in range(0, SSD_CONV_CH, cstep):
        y = _silu(_conv_cols(u_scr, cw_ref, cb_ref, tm, c0, c0 + cstep))
        if c0 < SSD_INNER:
            xc_ref[:, c0:c0 + cstep] = y
        elif c0 < SSD_INNER + 256:
            b_ref[...] = y
        else:
            c_ref[...] = y

    z_ref[...] = _dot(hm, wz_ref[...])
    dt_ref[...] = jax.nn.softplus(_dot(hm, wdt_ref[...]) + dtb_ref[...])
    sg_ref[...] = _silu(_dot(hm, wg_ref[...]))

    cos_t = cos_ref[...]
    sin_a = sa_ref[...]
    sin_b = sb_ref[...]

    def rope(t):
        return (t * cos_t + pltpu.roll(t, LANES - ROPE_DIM // 2, 1) * sin_a
                + pltpu.roll(t, ROPE_DIM // 2, 1) * sin_b)

    qf = _dot(hm, wq_ref[...]) * (ATTN_HEAD_DIM ** -0.5)
    for t in range(D_MODEL // LANES):
        q_ref[:, t * LANES:(t + 1) * LANES] = rope(qf[:, t * LANES:(t + 1) * LANES]).astype(BF16)

    low = lax.broadcasted_iota(jnp.int32, (tm, LANES), 1) < ATTN_HEAD_DIM
    kf = _dot(hm, wk_ref[...])
    vf = _dot(hm, wv_ref[...])
    for j in range(2):
        kt = rope(kf[:, j * LANES:(j + 1) * LANES])
        vt = vf[:, j * LANES:(j + 1) * LANES]
        for dst, a in ((kz_ref, kt), (vz_ref, vt)):
            r = pltpu.roll(a, ATTN_HEAD_DIM, 1)
            c = 4 * j * LANES
            dst[:, c:c + LANES] = jnp.where(low, a, 0.0).astype(BF16)
            dst[:, c + LANES:c + 2 * LANES] = jnp.where(low, 0.0, r).astype(BF16)
            dst[:, c + 2 * LANES:c + 3 * LANES] = jnp.where(low, r, 0.0).astype(BF16)
            dst[:, c + 3 * LANES:c + 4 * LANES] = jnp.where(low, 0.0, a).astype(BF16)


def _even_in(x, g, wz, wx, wdt, wq, wk, wv, wg, cw, cb, dtb, cos_t, sin_a, sin_b):
    n_rows = x.shape[0]
    tm = IN_TILE_EVEN
    rows = functools.partial(_row_spec, tm)
    in_specs = _halo_specs(tm, n_rows) + [
        _const_spec((1, D_MODEL)),
        _const_spec(wz.shape), _const_spec(wx.shape), _const_spec(wdt.shape), _const_spec(wq.shape),
        _const_spec(wk.shape), _const_spec(wv.shape), _const_spec(wg.shape),
        _const_spec(cw.shape), _const_spec(cb.shape), _const_spec(dtb.shape),
        rows(LANES), rows(LANES), rows(LANES),
    ]
    out_shape = (
        jax.ShapeDtypeStruct((n_rows, SSD_INNER), F32),
        jax.ShapeDtypeStruct((n_rows, SSD_INNER), F32),
        jax.ShapeDtypeStruct((n_rows, 256), F32),
        jax.ShapeDtypeStruct((n_rows, 256), F32),
        jax.ShapeDtypeStruct((n_rows, LANES), F32),
        jax.ShapeDtypeStruct((n_rows, D_MODEL), BF16),
        jax.ShapeDtypeStruct((n_rows, D_MODEL), BF16),
        jax.ShapeDtypeStruct((n_rows, D_MODEL), BF16),
        jax.ShapeDtypeStruct((n_rows, D_MODEL), F32),
    )
    out_specs = (rows(SSD_INNER), rows(SSD_INNER), rows(256), rows(256), rows(LANES),
                 rows(D_MODEL), rows(D_MODEL), rows(D_MODEL), rows(D_MODEL))
    return pl.pallas_call(
        functools.partial(_even_in_body, tm=tm),
        grid=(n_rows // tm,),
        in_specs=in_specs, out_specs=out_specs, out_shape=out_shape,
        scratch_shapes=[pltpu.VMEM((tm + 2 * HALO, D_MODEL), BF16),
                        pltpu.VMEM((tm + 2 * HALO, SSD_CONV_CH), F32)],
        compiler_params=_params(), name="even_in",
    )(x, x, x, g, wz, wx, wdt, wq, wk, wv, wg, cw, cb, dtb, cos_t, sin_a, sin_b)


def _ssd_body(*refs, reverse, combine):
    if combine:
        (xc_ref, b_ref, c_ref, dt_ref, alog_ref, e_ref, z_ref, yb_ref, dsk_ref, nw_ref,
         y_ref, state, ytmp) = refs
    else:
        xc_ref, b_ref, c_ref, dt_ref, alog_ref, e_ref, y_ref, state = refs
        ytmp = y_ref
    t_len = SSD_CHUNK
    off = SSD_HEADS if reverse else 0

    @pl.when(pl.program_id(0) == 0)
    def _():
        state[...] = jnp.zeros_like(state)

    row = lax.broadcasted_iota(jnp.int32, (t_len, t_len), 0)
    col = lax.broadcasted_iota(jnp.int32, (t_len, t_len), 1)
    mask = (col >= row) if reverse else (col <= row)
    tri = jnp.where(mask, 1.0, 0.0).astype(BF16)

    lane = lax.broadcasted_iota(jnp.int32, (1, LANES), 1)
    in_dir = (lane >= off) & (lane < off + SSD_HEADS)
    a_rate = jnp.where(in_dir, -jnp.exp(alog_ref[...]), 0.0)

    dt = dt_ref[...]
    cs = _dot3(tri, *_split3(dt * a_rate))
    log_dt = jnp.log(dt)
    cs_t = (cs - log_dt).T
    last = 0 if reverse else t_len - 1
    cs_last = cs[last:last + 1, :]
    expand = e_ref[...]
    ecs_x = _dot3(*_split3(jnp.exp(cs)), expand, rhs_split=False)
    wdt_x = _dot3(*_split3(jnp.exp(cs_last - cs + log_dt)), expand, rhs_split=False)

    xc = xc_ref[...]
    xw_b = (xc * wdt_x).astype(BF16)
    b_f = b_ref[...]
    c_b = c_ref[...].astype(BF16)
    b_b = b_f.astype(BF16)
    b_t = b_f.T.astype(BF16)
    low = lax.broadcasted_iota(jnp.int32, (t_len, LANES), 1) < SSD_HEAD_DIM

    gw = SSD_INNER // SSD_GROUPS
    for g in range(SSD_GROUPS):
        c_g = c_b[:, g * SSD_STATE:(g + 1) * SSD_STATE]
        b_g = b_b[:, g * SSD_STATE:(g + 1) * SSD_STATE]
        cb = lax.dot_general(c_g, b_g, (((1,), (1,)), ((), ())), preferred_element_type=F32)
        s_g = state[:, g * gw:(g + 1) * gw]
        y_off = _dot(c_g, s_g.astype(BF16)) * ecs_x[:, g * gw:(g + 1) * gw]
        for tt in range(gw // LANES):
            t = g * (gw // LANES) + tt
            xt = xc[:, t * LANES:(t + 1) * LANES]
            acc = y_off[:, tt * LANES:(tt + 1) * LANES]
            for half in range(2):
                ln = off + 2 * t + half
                seg = cs[:, ln:ln + 1] - cs_t[ln:ln + 1, :]
                decay = jnp.exp(jnp.where(mask, seg, NEG))
                m = (cb * decay).astype(BF16)
                xh = jnp.where(low, xt, 0.0) if half == 0 else jnp.where(low, 0.0, xt)
                acc = acc + _dot(m, xh.astype(BF16))
            ytmp[:, t * LANES:(t + 1) * LANES] = acc
        state[:, g * gw:(g + 1) * gw] = (
            ecs_x[last:last + 1, g * gw:(g + 1) * gw] * s_g
            + _dot(b_t[g * SSD_STATE:(g + 1) * SSD_STATE, :], xw_b[:, g * gw:(g + 1) * gw]))

    if combine:
        y = ytmp[...] + yb_ref[...] + xc * dsk_ref[...]
        y = y * _silu(z_ref[...])
        for g in range(SSD_GROUPS):
            yg = y[:, g * gw:(g + 1) * gw]
            ms = jnp.mean(yg * yg, axis=-1, keepdims=True)
            y_ref[:, g * gw:(g + 1) * gw] = (
                yg * lax.rsqrt(ms + EPS) * nw_ref[:, g * gw:(g + 1) * gw]).astype(BF16)


def _ssd(xc, bm, cm, dt, alog_row, expand, reverse, combine_args=None):
    n_rows = xc.shape[0]
    t_len = SSD_CHUNK
    nc = n_rows // t_len
    rows = functools.partial(_row_spec, t_len, rev_n=nc if reverse else None)
    combine = combine_args is not None
    in_specs = [rows(SSD_INNER), rows(256), rows(256), rows(LANES),
                _const_spec((1, LANES)), _const_spec((LANES, SSD_INNER))]
    args = [xc, bm, cm, dt, alog_row, expand]
    scratch = [pltpu.VMEM((SSD_STATE, SSD_INNER), F32)]
    if combine:
        z, y_bwd, dskip_x, norm_w = combine_args
        in_specs += [rows(SSD_INNER), rows(SSD_INNER),
                     _const_spec((1, SSD_INNER)), _const_spec((1, SSD_INNER))]
        args += [z, y_bwd, dskip_x, norm_w]
        scratch.append(pltpu.VMEM((t_len, SSD_INNER), F32))
        out_dtype = BF16
    else:
        out_dtype = F32
    return pl.pallas_call(
        functools.partial(_ssd_body, reverse=reverse, combine=combine),
        grid=(nc,),
        in_specs=in_specs, out_specs=rows(SSD_INNER),
        out_shape=jax.ShapeDtypeStruct((n_rows, SSD_INNER), out_dtype),
        scratch_shapes=scratch,
        compiler_params=_params(), name="ssd_fwd" if combine else "ssd_bwd",
    )(*args)


def _attn_body(sink_ref, q_ref, kp_ref, km_ref, kn_ref, vp_ref, vm_ref, vn_ref, sg_ref, ys_ref,
               x_ref, w1_ref, w2_ref, o_ref, k_scr, v_scr, y_scr, s_scr, *, tq):
    i = pl.program_id(0)
    n = pl.num_programs(0)
    blk = ATTN_BLOCK
    nq = tq // blk
    for scr, p_ref, m_ref, n_ref in ((k_scr, kp_ref, km_ref, kn_ref), (v_scr, vp_ref, vm_ref, vn_ref)):
        scr[0:blk, :] = p_ref[...]
        scr[blk:blk + tq, :] = m_ref[...]
        scr[blk + tq:blk + tq + blk, :] = n_ref[...]

    row2 = lax.broadcasted_iota(jnp.int32, (2 * blk, blk), 0)
    qi = row2 & (blk - 1)
    kj = lax.broadcasted_iota(jnp.int32, (2 * blk, blk), 1)
    first_tile = row2[:, 0:1] < blk
    n_kv = ATTN_KV_HEADS

    def one_block(b, carry):
        r0 = pl.multiple_of(b * blk, blk)
        gb = i * nq + b
        bias_l = jnp.where((kj >= qi) & (gb > 0), 0.0, NEG)
        bias_r = jnp.where((kj <= qi) & (gb < n * nq - 1), 0.0, NEG)
        for hk in range(n_kv):
            q2 = jnp.concatenate(
                [q_ref[pl.ds(r0, blk), (2 * hk) * LANES:(2 * hk + 1) * LANES],
                 q_ref[pl.ds(r0, blk), (2 * hk + 1) * LANES:(2 * hk + 2) * LANES]], axis=0)
            for half in range(2):
                c = (2 * hk + half) * LANES
                kz = k_scr[pl.ds(r0, 3 * blk), c:c + LANES]
                s_scr[2 * hk + half] = lax.dot_general(
                    q2, kz, (((1,), (1,)), ((), ())), preferred_element_type=F32)
        for hk in range(n_kv):
            out = None
            for half in range(2):
                c = (2 * hk + half) * LANES
                s = s_scr[2 * hk + half]
                s_l = s[:, 0:blk] + bias_l
                s_m = s[:, blk:2 * blk]
                s_r = s[:, 2 * blk:3 * blk] + bias_r
                sink = jnp.where(first_tile, sink_ref[4 * hk + half], sink_ref[4 * hk + 2 + half])
                m = jnp.max(jnp.maximum(jnp.maximum(s_l, s_m), s_r), axis=-1, keepdims=True)
                m = jnp.maximum(m, sink)
                p_l = jnp.exp(s_l - m)
                p_m = jnp.exp(s_m - m)
                p_r = jnp.exp(s_r - m)
                den = jnp.sum(p_l + p_m + p_r, axis=-1, keepdims=True) + jnp.exp(sink - m)
                p = jnp.concatenate([p_l, p_m, p_r], axis=1).astype(BF16)
                o = _dot(p, v_scr[pl.ds(r0, 3 * blk), c:c + LANES]) * (1.0 / den)
                out = o if out is None else out + o
            for k in range(2):
                t = 2 * hk + k
                y_scr[pl.ds(r0, blk), t * LANES:(t + 1) * LANES] = (
                    out[k * blk:(k + 1) * blk, :]
                    * sg_ref[pl.ds(r0, blk), t * LANES:(t + 1) * LANES]).astype(BF16)
        return carry

    lax.fori_loop(0, nq, one_block, 0)
    o_ref[...] = x_ref[...] + _dot(ys_ref[...], w1_ref[...]) + _dot(y_scr[...], w2_ref[...])


def _attn_out(sink, q, kz, vz, sg, y_ssd, x, w1, w2):
    n_rows = x.shape[0]
    tq = ATTN_TILE
    blk = ATTN_BLOCK
    per = tq // blk
    nb = n_rows // blk
    rows = functools.partial(_row_spec, tq)
    prev = pl.BlockSpec((blk, D_MODEL), lambda i: (jnp.maximum(i * per - 1, 0), 0))
    nxt = pl.BlockSpec((blk, D_MODEL), lambda i: (jnp.minimum((i + 1) * per, nb - 1), 0))
    in_specs = [pl.BlockSpec(memory_space=pltpu.SMEM),
                rows(D_MODEL), prev, rows(D_MODEL), nxt, prev, rows(D_MODEL), nxt,
                rows(D_MODEL), rows(D_MODEL), rows(D_MODEL),
                _const_spec(w1.shape), _const_spec(w2.shape)]
    return pl.pallas_call(
        functools.partial(_attn_body, tq=tq),
        grid=(n_rows // tq,),
        in_specs=in_specs, out_specs=rows(D_MODEL),
        out_shape=jax.ShapeDtypeStruct((n_rows, D_MODEL), F32),
        scratch_shapes=[pltpu.VMEM((tq + 2 * blk, D_MODEL), BF16),
                        pltpu.VMEM((tq + 2 * blk, D_MODEL), BF16),
                        pltpu.VMEM((tq, D_MODEL), BF16),
                        pltpu.VMEM((2 * ATTN_KV_HEADS, 2 * blk, 3 * blk), F32)],
        compiler_params=_params(), name="attn_out",
    )(sink, q, kz, kz, kz, vz, vz, vz, sg, y_ssd, x, w1, w2)


def _odd_in_body(xp_ref, xm_ref, xn_ref, g_ref, wx_ref, wg_ref, cw_ref, cb_ref,
                 xc_ref, sg_ref, h_scr, u_scr, *, tm):
    _fill_normed(xp_ref, xm_ref, xn_ref, g_ref, h_scr, tm)
    u_scr[...] = _dot(h_scr[...], wx_ref[...])
    cstep = 256
    for c0 in range(0, LRU_WIDTH, cstep):
        xc_ref[:, c0:c0 + cstep] = _conv_cols(u_scr, cw_ref, cb_ref, tm, c0, c0 + cstep)
    sg_ref[...] = _silu(_dot(h_scr[HALO:HALO + tm, :], wg_ref[...]))


def _odd_in(x, g, wx, wg, cw, cb):
    n_rows = x.shape[0]
    tm = IN_TILE_ODD
    rows = functools.partial(_row_spec, tm)
    in_specs = _halo_specs(tm, n_rows) + [
        _const_spec((1, D_MODEL)), _const_spec(wx.shape), _const_spec(wg.shape),
        _const_spec(cw.shape), _const_spec(cb.shape)]
    return pl.pallas_call(
        functools.partial(_odd_in_body, tm=tm),
        grid=(n_rows // tm,),
        in_specs=in_specs, out_specs=(rows(LRU_WIDTH), rows(LRU_WIDTH)),
        out_shape=(jax.ShapeDtypeStruct((n_rows, LRU_WIDTH), F32),
                   jax.ShapeDtypeStruct((n_rows, LRU_WIDTH), F32)),
        scratch_shapes=[pltpu.VMEM((tm + 2 * HALO, D_MODEL), BF16),
                        pltpu.VMEM((tm + 2 * HALO, LRU_WIDTH), F32)],
        compiler_params=_params(), name="odd_in",
    )(x, x, x, g, wx, wg, cw, cb)


def _lru_body(*refs, reverse, combine, final):
    if combine:
        if final:
            (xc_ref, w_ref, ba_ref, bx_ref, lam_ref, hb_ref, sg_ref, x_ref, wo_ref, fw_ref,
             o_ref, a_scr, u_scr, carry, y_scr) = refs
        else:
            (xc_ref, w_ref, ba_ref, bx_ref, lam_ref, hb_ref, sg_ref, x_ref, wo_ref,
             o_ref, a_scr, u_scr, carry, y_scr) = refs
    else:
        xc_ref, w_ref, ba_ref, bx_ref, lam_ref, o_ref, a_scr, u_scr, carry = refs
    sub, pitch = LRU_SUB, LRU_PITCH
    n_grp = LRU_WIDTH // LANES

    @pl.when(pl.program_id(0) == 0)
    def _():
        carry[...] = jnp.zeros_like(carry)

    rate = -LRU_C * jax.nn.softplus(-lam_ref[...])
    for blk in range(LRU_BLOCKS):
        c0 = blk * LRU_BLOCK
        xb = xc_ref[:, c0:c0 + LRU_BLOCK]
        ri = _dot(xb.astype(BF16), w_ref[blk])
        r = jax.nn.sigmoid(ri[:, :LRU_BLOCK] + ba_ref[:, c0:c0 + LRU_BLOCK])
        ig = jax.nn.sigmoid(ri[:, LRU_BLOCK:] + bx_ref[:, c0:c0 + LRU_BLOCK])
        log_a = r * rate[:, c0:c0 + LRU_BLOCK]
        a = jnp.exp(log_a)
        th = jnp.tanh(log_a)
        num = -2.0 * th
        root = jnp.where(num > 0.0, num * lax.rsqrt(num), 0.0) * lax.rsqrt(1.0 - th)
        u = root * (ig * xb)
        for half in range(LRU_BLOCK // LANES):
            gi = blk * (LRU_BLOCK // LANES) + half
            for s in range(SUBLANES):
                a_scr[gi, s * pitch:s * pitch + sub, :] = a[s * sub:(s + 1) * sub, half * LANES:(half + 1) * LANES]
                u_scr[gi, s * pitch:s * pitch + sub, :] = u[s * sub:(s + 1) * sub, half * LANES:(half + 1) * LANES]

    steps = list(range(sub))
    if reverse:
        steps = steps[::-1]
    j0 = steps[0]
    h = u_scr[:, pl.ds(j0, SUBLANES, stride=pitch), :]
    p = a_scr[:, pl.ds(j0, SUBLANES, stride=pitch), :]
    for j in steps[1:]:
        aj = a_scr[:, pl.ds(j, SUBLANES, stride=pitch), :]
        uj = u_scr[:, pl.ds(j, SUBLANES, stride=pitch), :]
        h = aj * h + uj
        p = aj * p
        u_scr[:, pl.ds(j, SUBLANES, stride=pitch), :] = h
        a_scr[:, pl.ds(j, SUBLANES, stride=pitch), :] = p

    c = carry[...]
    order = list(range(SUBLANES))
    if reverse:
        order = order[::-1]
    carry_in = [None] * SUBLANES
    for s in order:
        carry_in[s] = c
        c = p[:, s:s + 1, :] * c + h[:, s:s + 1, :]
    carry[...] = c

    for s in range(SUBLANES):
        hs = (u_scr[:, s * pitch:s * pitch + sub, :]
              + a_scr[:, s * pitch:s * pitch + sub, :] * carry_in[s])
        for gi in range(n_grp):
            rs = slice(s * sub, (s + 1) * sub)
            cs_ = slice(gi * LANES, (gi + 1) * LANES)
            if combine:
                y_scr[rs, cs_] = ((hs[gi] + hb_ref[rs, cs_]) * sg_ref[rs, cs_]).astype(BF16)
            else:
                o_ref[rs, cs_] = hs[gi]

    if combine:
        out = x_ref[...] + _dot(y_scr[...], wo_ref[...])
        if final:
            out = _rms(out, fw_ref[...])
        o_ref[...] = out


def _lru(xc, w_gates, b_a, b_x, lam, reverse, combine_args=None, final_w=None):
    n_rows = xc.shape[0]
    tm = LRU_TILE
    n = n_rows // tm
    rows = functools.partial(_row_spec, tm, rev_n=n if reverse else None)
    combine = combine_args is not None
    final = final_w is not None
    in_specs = [rows(LRU_WIDTH), _const_spec(w_gates.shape), _const_spec((1, LRU_WIDTH)),
                _const_spec((1, LRU_WIDTH)), _const_spec((1, LRU_WIDTH))]
    args = [xc, w_gates, b_a, b_x, lam]
    n_grp = LRU_WIDTH // LANES
    scratch = [pltpu.VMEM((n_grp, SUBLANES * LRU_PITCH, LANES), F32),
               pltpu.VMEM((n_grp, SUBLANES * LRU_PITCH, LANES), F32),
               pltpu.VMEM((n_grp, 1, LANES), F32)]
    if combine:
        h_bwd, sg, x, w_out = combine_args
        in_specs += [rows(LRU_WIDTH), rows(LRU_WIDTH), rows(D_MODEL), _const_spec(w_out.shape)]
        args += [h_bwd, sg, x, w_out]
        if final:
            in_specs.append(_const_spec((1, D_MODEL)))
            args.append(final_w)
        scratch.append(pltpu.VMEM((tm, LRU_WIDTH), BF16))
        out_w = D_MODEL
    else:
        out_w = LRU_WIDTH
    return pl.pallas_call(
        functools.partial(_lru_body, reverse=reverse, combine=combine, final=final),
        grid=(n,),
        in_specs=in_specs, out_specs=rows(out_w),
        out_shape=jax.ShapeDtypeStruct((n_rows, out_w), F32),
        scratch_shapes=scratch,
        compiler_params=_params(), name="lru_fwd" if combine else "lru_bwd",
    )(*args)


def _rope_tables(n_rows):
    inv = ROPE_THETA ** (-jnp.arange(0, ROPE_DIM, 2, dtype=F32) / ROPE_DIM)
    ang = jnp.arange(n_rows, dtype=F32)[:, None] * inv[None, :]
    cos, sin = jnp.cos(ang), jnp.sin(ang)
    half = ROPE_DIM // 2
    ones = jnp.ones((n_rows, ATTN_HEAD_DIM - ROPE_DIM), F32)
    zeros = jnp.zeros((n_rows, ATTN_HEAD_DIM - ROPE_DIM), F32)
    zero_h = jnp.zeros((n_rows, half), F32)
    reps = LANES // ATTN_HEAD_DIM
    cos_t = jnp.tile(jnp.concatenate([cos, cos, ones], axis=1), (1, reps))
    sin_a = jnp.tile(jnp.concatenate([-sin, zero_h, zeros], axis=1), (1, reps))
    sin_b = jnp.tile(jnp.concatenate([zero_h, sin, zeros], axis=1), (1, reps))
    return cos_t, sin_a, sin_b


def _pad_lanes(v):
    return jnp.pad(v.astype(F32), (0, LANES - v.shape[0]))[None, :]


def _even_layer(x, g_norm, w_in, conv_w, conv_b, dt_bias, a_log, d_skip, ssd_norm_w, sink, w_out, rope):
    sizes = (SSD_INNER, SSD_CONV_CH, 2 * SSD_HEADS, D_MODEL, 256, 256, D_MODEL)
    cuts = [0]
    for s in sizes:
        cuts.append(cuts[-1] + s)
    wz, wx, wdt, wq, wk, wv, wg = (w_in[:, cuts[k]:cuts[k + 1]].astype(BF16) for k in range(7))
    wdt = jnp.pad(wdt, ((0, 0), (0, LANES - 2 * SSD_HEADS)))
    dtb = _pad_lanes(dt_bias.reshape(-1))
    z, xc, bm, cm, dt, q, kz, vz, sg = _even_in(
        x, g_norm[None, :], wz, wx, wdt, wq, wk, wv, wg, conv_w, conv_b[None, :], dtb, *rope)

    lane = jnp.arange(LANES)[:, None]
    head = (jnp.arange(SSD_INNER) // SSD_HEAD_DIM)[None, :]
    alog_row = _pad_lanes(a_log.reshape(-1))
    y_bwd = _ssd(xc, bm, cm, dt, alog_row, (lane == head + SSD_HEADS).astype(BF16), reverse=True)
    dskip_x = jnp.repeat(d_skip.astype(F32), SSD_HEAD_DIM)[None, :]
    y_ssd = _ssd(xc, bm, cm, dt, alog_row, (lane == head).astype(BF16), reverse=False,
                 combine_args=(z, y_bwd, dskip_x, ssd_norm_w[None, :]))

    w_out_b = w_out.astype(BF16)
    return _attn_out(sink.astype(F32), q, kz, vz, sg, y_ssd, x, w_out_b[:SSD_INNER], w_out_b[SSD_INNER:])


def _odd_layer(x, g_norm, w_in, conv_w, conv_b, w_a, b_a, w_x, b_x, lam, w_out, final_w):
    w_in_b = w_in.astype(BF16)
    xc, sg = _odd_in(x, g_norm[None, :], w_in_b[:, :LRU_WIDTH], w_in_b[:, LRU_WIDTH:],
                     conv_w, conv_b[None, :])
    w_gates = jnp.concatenate([w_a, w_x], axis=-1).astype(BF16)
    h_bwd = _lru(xc, w_gates[1], b_a[1][None, :], b_x[1][None, :], lam[1][None, :], reverse=True)
    return _lru(xc, w_gates[0], b_a[0][None, :], b_x[0][None, :], lam[0][None, :], reverse=False,
                combine_args=(h_bwd, sg, x, w_out.astype(BF16)),
                final_w=None if final_w is None else final_w[None, :])


def kernel(x, norm_w, final_norm_w, ev_w_in, ev_conv_w, ev_conv_b, ev_dt_bias, ev_a_log, ev_d_skip, ev_ssd_norm_w, ev_sink, ev_w_out, od_w_in, od_conv_w, od_conv_b, od_w_a, od_b_a, od_w_x, od_b_x, od_lambda, od_w_out):
    batch, n_rows, _ = x.shape
    depth = norm_w.shape[0]
    assert batch == 1 and depth % 2 == 0
    rope = _rope_tables(n_rows)
    h = x[0]
    for layer in range(depth):
        j = layer // 2
        if layer % 2 == 0:
            h = _even_layer(h, norm_w[layer], ev_w_in[j], ev_conv_w[j], ev_conv_b[j], ev_dt_bias[j],
                            ev_a_log[j], ev_d_skip[j], ev_ssd_norm_w[j], ev_sink[j], ev_w_out[j], rope)
        else:
            h = _odd_layer(h, norm_w[layer], od_w_in[j], od_conv_w[j], od_conv_b[j], od_w_a[j],
                           od_b_a[j], od_w_x[j], od_b_x[j], od_lambda[j], od_w_out[j],
                           final_norm_w if layer == depth - 1 else None)
    return h[None]
```

```python
import functools

import jax
import jax.numpy as jnp
from jax import lax
from jax.experimental import pallas as pl
from jax.experimental.pallas import tpu as pltpu

F32 = jnp.float32
BF16 = jnp.bfloat16

D_MODEL = 1024
EPS = 1e-6
LANES = 128
SUBLANES = 8
HALO = 16
CONV_W = 4
CONV_LEFT = 2

SSD_HEADS = 16
SSD_HEAD_DIM = 64
SSD_STATE = 128
SSD_GROUPS = 2
SSD_CHUNK = 256
SSD_INNER = 1024
SSD_CONV_CH = 1536

ATTN_HEADS = 16
ATTN_KV_HEADS = 4
ATTN_HEAD_DIM = 64
ATTN_BLOCK = 128
ROPE_THETA = 500000.0
ROPE_DIM = 16

LRU_WIDTH = 2048
LRU_BLOCK = 256
LRU_BLOCKS = 8
LRU_C = 8.0

NEG = -1e30
VMEM_LIMIT = 56 * 1024 * 1024

IN_TILE_EVEN = 512
IN_TILE_ODD = 512
ATTN_TILE = 512
LRU_TILE = 256
LRU_SUB = LRU_TILE // SUBLANES
LRU_PITCH = LRU_SUB + SUBLANES
LRU_HALF_GROUPS = 8


def _dot(a, b):
    return jnp.dot(a, b, preferred_element_type=F32)


def _split3(v):
    hi = v.astype(BF16)
    r1 = v - hi.astype(F32)
    mid = r1.astype(BF16)
    lo = (r1 - mid.astype(F32)).astype(BF16)
    return hi, mid, lo


def _dot3(*ops, rhs_split=True):
    if rhs_split:
        m, parts = ops[0], ops[1:]
        return _dot(m, parts[0]) + _dot(m, parts[1]) + _dot(m, parts[2])
    parts, m = ops[:3], ops[3]
    return _dot(parts[0], m) + _dot(parts[1], m) + _dot(parts[2], m)


def _rms(x, g):
    ms = jnp.mean(x * x, axis=-1, keepdims=True)
    return x * lax.rsqrt(ms + EPS) * g


def _sigmoid(x):
    return 0.5 * jnp.tanh(0.5 * x) + 0.5


def _silu(x):
    h = 0.5 * x
    return h + h * jnp.tanh(h)


def _params(n_axes=1):
    return pltpu.CompilerParams(dimension_semantics=("arbitrary",) * n_axes,
                                vmem_limit_bytes=VMEM_LIMIT)


def _const_spec(shape):
    nd = len(shape)
    return pl.BlockSpec(shape, lambda i: (0,) * nd, pipeline_mode=pl.Buffered(1))


def _row_spec(tm, width, rev_n=None):
    if rev_n is None:
        return pl.BlockSpec((tm, width), lambda i: (i, 0))
    return pl.BlockSpec((tm, width), lambda i: (rev_n - 1 - i, 0))


def _halo_specs(tm, n_rows):
    hb = tm // HALO
    nh = n_rows // HALO
    prev = pl.BlockSpec((HALO, D_MODEL), lambda i: (jnp.maximum(i * hb - 1, 0), 0))
    main = pl.BlockSpec((tm, D_MODEL), lambda i: (i, 0))
    nxt = pl.BlockSpec((HALO, D_MODEL), lambda i: (jnp.minimum((i + 1) * hb, nh - 1), 0))
    return [prev, main, nxt]


def _fill_normed(xp_ref, xm_ref, xn_ref, g_ref, h_scr, tm):
    i = pl.program_id(0)
    n = pl.num_programs(0)
    g = g_ref[...]
    hp = _rms(xp_ref[...], g)
    hn = _rms(xn_ref[...], g)
    h_scr[0:HALO, :] = jnp.where(i > 0, hp, 0.0).astype(BF16)
    h_scr[HALO:HALO + tm, :] = _rms(xm_ref[...], g).astype(BF16)
    h_scr[HALO + tm:HALO + tm + HALO, :] = jnp.where(i < n - 1, hn, 0.0).astype(BF16)


def _proj_conv(h_ext, w_ref, u_scr, cw_ref, cb_ref, tm, c0, width):
    u = _dot(h_ext, w_ref[:, c0:c0 + width])
    base = HALO - CONV_LEFT
    outs = []
    for j in range(width // LANES):
        slab = c0 // LANES + j
        u_scr[slab] = u[:, j * LANES:(j + 1) * LANES]
        cs_ = slice(c0 + j * LANES, c0 + (j + 1) * LANES)
        acc = cb_ref[:, cs_] + cw_ref[0:1, cs_] * u_scr[slab, base:base + tm, :]
        for k in range(1, CONV_W):
            acc = acc + cw_ref[k:k + 1, cs_] * u_scr[slab, base + k:base + k + tm, :]
        outs.append(acc)
    return outs


def _even_in_body(xp_ref, xm_ref, xn_ref, g_ref, wz_ref, wx_ref, wdt_ref, wq_ref, wk_ref, wv_ref,
                  wg_ref, cw_ref, cb_ref, dtb_ref, cos_ref, sa_ref, sb_ref,
                  z_ref, xc_ref, b_ref, c_ref, dt_ref, q_ref, kz_ref, vz_ref, sg_ref,
                  h_scr, u_scr, *, tm):
    _fill_normed(xp_ref, xm_ref, xn_ref, g_ref, h_scr, tm)
    h_ext = h_scr[...]
    hm = h_scr[HALO:HALO + tm, :]
    cstep = 256
    cos_t = cos_ref[...]
    sin_a = sa_ref[...]
    sin_b = sb_ref[...]
    low = lax.broadcasted_iota(jnp.int32, (tm, LANES), 1) < ATTN_HEAD_DIM

    def rope(t):
        return (t * cos_t + pltpu.roll(t, LANES - ROPE_DIM // 2, 1) * sin_a
                + pltpu.roll(t, ROPE_DIM // 2, 1) * sin_b)

    def do_xbc(c0):
        outs = _proj_conv(h_ext, wx_ref, u_scr, cw_ref, cb_ref, tm, c0, cstep)
        for j, acc in enumerate(outs):
            y = _silu(acc)
            c = c0 + j * LANES
            if c < SSD_INNER:
                xc_ref[:, c:c + LANES] = y
            elif c < SSD_INNER + 256:
                b_ref[:, c - SSD_INNER:c - SSD_INNER + LANES] = y
            else:
                c_ref[:, c - SSD_INNER - 256:c - SSD_INNER - 256 + LANES] = y

    def do_z(c0):
        z_ref[:, c0:c0 + cstep] = _dot(hm, wz_ref[:, c0:c0 + cstep])

    def do_gate(c0):
        sg_ref[:, c0:c0 + cstep] = _silu(_dot(hm, wg_ref[:, c0:c0 + cstep]))

    def do_q(c0):
        qf = _dot(hm, wq_ref[:, c0:c0 + cstep]) * (ATTN_HEAD_DIM ** -0.5)
        for j in range(cstep // LANES):
            q_ref[:, c0 + j * LANES:c0 + (j + 1) * LANES] = rope(qf[:, j * LANES:(j + 1) * LANES]).astype(BF16)

    def do_kv(w_ref, dst, rotary):
        f = _dot(hm, w_ref[...])
        for j in range(2):
            a = f[:, j * LANES:(j + 1) * LANES]
            if rotary:
                a = rope(a)
            r = pltpu.roll(a, ATTN_HEAD_DIM, 1)
            c = 4 * j * LANES
            dst[:, c:c + LANES] = jnp.where(low, a, 0.0).astype(BF16)
            dst[:, c + LANES:c + 2 * LANES] = jnp.where(low, 0.0, r).astype(BF16)
            dst[:, c + 2 * LANES:c + 3 * LANES] = jnp.where(low, r, 0.0).astype(BF16)
            dst[:, c + 3 * LANES:c + 4 * LANES] = jnp.where(low, 0.0, a).astype(BF16)

    for n in range(4):
        do_xbc(n * cstep)
        do_z(n * cstep)
        do_q(n * cstep)
        do_gate(n * cstep)
        if n == 0:
            do_xbc(4 * cstep)
        elif n == 1:
            do_xbc(5 * cstep)
        elif n == 2:
            do_kv(wk_ref, kz_ref, True)
        else:
            do_kv(wv_ref, vz_ref, False)
    dt_ref[...] = jax.nn.softplus(_dot(hm, wdt_ref[...]) + dtb_ref[...])


def _even_in(x, g, wz, wx, wdt, wq, wk, wv, wg, cw, cb, dtb, cos_t, sin_a, sin_b):
    n_rows = x.shape[0]
    tm = IN_TILE_EVEN
    rows = functools.partial(_row_spec, tm)
    in_specs = _halo_specs(tm, n_rows) + [
        _const_spec((1, D_MODEL)),
        _const_spec(wz.shape), _const_spec(wx.shape), _const_spec(wdt.shape), _const_spec(wq.shape),
        _const_spec(wk.shape), _const_spec(wv.shape), _const_spec(wg.shape),
        _const_spec(cw.shape), _const_spec(cb.shape), _const_spec(dtb.shape),
        rows(LANES), rows(LANES), rows(LANES),
    ]
    out_shape = (
        jax.ShapeDtypeStruct((n_rows, SSD_INNER), F32),
        jax.ShapeDtypeStruct((n_rows, SSD_INNER), F32),
        jax.ShapeDtypeStruct((n_rows, 256), F32),
        jax.ShapeDtypeStruct((n_rows, 256), F32),
        jax.ShapeDtypeStruct((n_rows, LANES), F32),
        jax.ShapeDtypeStruct((n_rows, D_MODEL), BF16),
        jax.ShapeDtypeStruct((n_rows, D_MODEL), BF16),
        jax.ShapeDtypeStruct((n_rows, D_MODEL), BF16),
        jax.ShapeDtypeStruct((n_rows, D_MODEL), F32),
    )
    out_specs = (rows(SSD_INNER), rows(SSD_INNER), rows(256), rows(256), rows(LANES),
                 rows(D_MODEL), rows(D_MODEL), rows(D_MODEL), rows(D_MODEL))
    return pl.pallas_call(
        functools.partial(_even_in_body, tm=tm),
        grid=(n_rows // tm,),
        in_specs=in_specs, out_specs=out_specs, out_shape=out_shape,
        scratch_shapes=[pltpu.VMEM((tm + 2 * HALO, D_MODEL), BF16),
                        pltpu.VMEM((SSD_CONV_CH // LANES, tm + 2 * HALO, LANES), F32)],
        compiler_params=_params(), name="even_in",
    )(x, x, x, g, wz, wx, wdt, wq, wk, wv, wg, cw, cb, dtb, cos_t, sin_a, sin_b)


def _ssd_body(*refs, reverse, combine):
    if combine:
        (xc_ref, b_ref, c_ref, dt_ref, alog_ref, e_ref, z_ref, yb_ref, dsk_ref, nw_ref,
         y_ref, state, cb_scr, yoff_scr, ytmp) = refs
    else:
        xc_ref, b_ref, c_ref, dt_ref, alog_ref, e_ref, y_ref, state, cb_scr, yoff_scr = refs
        ytmp = y_ref
    t_len = SSD_CHUNK
    off = SSD_HEADS if reverse else 0

    @pl.when(pl.program_id(0) == 0)
    def _():
        state[...] = jnp.zeros_like(state)

    row = lax.broadcasted_iota(jnp.int32, (t_len, t_len), 0)
    col = lax.broadcasted_iota(jnp.int32, (t_len, t_len), 1)
    mask = (col >= row) if reverse else (col <= row)
    tri = jnp.where(mask, 1.0, 0.0).astype(BF16)

    lane = lax.broadcasted_iota(jnp.int32, (1, LANES), 1)
    in_dir = (lane >= off) & (lane < off + SSD_HEADS)
    a_rate = jnp.where(in_dir, -jnp.exp(alog_ref[...]), 0.0)

    dt = dt_ref[...]
    cs = _dot3(tri, *_split3(dt * a_rate))
    log_dt = jnp.log(dt)
    cs_t = (cs - log_dt).T
    last = 0 if reverse else t_len - 1
    cs_last = cs[last:last + 1, :]
    w_hi, w_mid, _ = _split3(jnp.exp(cs_last - cs + log_dt))
    expand = e_ref[...]
    wdt_x = _dot(w_hi, expand) + _dot(w_mid, expand)

    b_f = b_ref[...]
    c_b = c_ref[...].astype(BF16)
    b_b = b_f.astype(BF16)
    b_t = b_f.T.astype(BF16)
    low = lax.broadcasted_iota(jnp.int32, (t_len, LANES), 1) < SSD_HEAD_DIM

    hb = t_len // 2
    key_blocks = ((0, 1), (1,)) if reverse else ((0,), (0, 1))
    diag_mask = mask[0:hb, 0:hb]

    gw = SSD_INNER // SSD_GROUPS
    for g in range(SSD_GROUPS):
        c_g = c_b[:, g * SSD_STATE:(g + 1) * SSD_STATE]
        b_g = b_b[:, g * SSD_STATE:(g + 1) * SSD_STATE]
        cb_scr[...] = lax.dot_general(c_g, b_g, (((1,), (1,)), ((), ())), preferred_element_type=F32)
        s_g = state[:, g * gw:(g + 1) * gw]
        yoff_scr[...] = _dot(c_g, s_g.astype(BF16))
        edge_decay = []
        for tt in range(gw // LANES):
            t = g * (gw // LANES) + tt
            ts = slice(t * LANES, (t + 1) * LANES)
            xt = xc_ref[:, ts]
            x_halves = (jnp.where(low, xt, 0.0).astype(BF16), jnp.where(low, 0.0, xt).astype(BF16))
            cols = [jnp.broadcast_to(cs[:, off + 2 * t + half:off + 2 * t + half + 1], (t_len, LANES))
                    for half in range(2)]
            e_tile = jnp.where(low, jnp.exp(cols[0]), jnp.exp(cols[1]))
            for rb in range(2):
                rs = slice(rb * hb, (rb + 1) * hb)
                acc = yoff_scr[rs, tt * LANES:(tt + 1) * LANES] * e_tile[rs, :]
                for half in range(2):
                    ln = off + 2 * t + half
                    pieces = []
                    for kb in key_blocks[rb]:
                        ks = slice(kb * hb, (kb + 1) * hb)
                        seg = cols[half][rs, :] - cs_t[ln:ln + 1, ks]
                        if kb == rb:
                            seg = jnp.where(diag_mask, seg, NEG)
                        pieces.append((cb_scr[rs, ks] * jnp.exp(seg)).astype(BF16))
                    m = pieces[0] if len(pieces) == 1 else jnp.concatenate(pieces, axis=1)
                    k0, k1 = key_blocks[rb][0] * hb, (key_blocks[rb][-1] + 1) * hb
                    acc = acc + _dot(m, x_halves[half][k0:k1, :])
                ytmp[rs, ts] = acc
            edge_decay.append(e_tile[last:last + 1, :])
        gs = slice(g * gw, (g + 1) * gw)
        xw_b = (xc_ref[:, gs] * wdt_x[:, gs]).astype(BF16)
        state[:, gs] = (jnp.concatenate(edge_decay, axis=1) * s_g
                        + _dot(b_t[g * SSD_STATE:(g + 1) * SSD_STATE, :], xw_b))

    if combine:
        y = ytmp[...] + yb_ref[...] + xc_ref[...] * dsk_ref[...]
        y = y * _silu(z_ref[...])
        for g in range(SSD_GROUPS):
            yg = y[:, g * gw:(g + 1) * gw]
            ms = jnp.mean(yg * yg, axis=-1, keepdims=True)
            y_ref[:, g * gw:(g + 1) * gw] = (
                yg * lax.rsqrt(ms + EPS) * nw_ref[:, g * gw:(g + 1) * gw]).astype(BF16)


def _ssd(xc, bm, cm, dt, alog_row, expand, reverse, combine_args=None):
    n_rows = xc.shape[0]
    t_len = SSD_CHUNK
    nc = n_rows // t_len
    rows = functools.partial(_row_spec, t_len, rev_n=nc if reverse else None)
    combine = combine_args is not None
    in_specs = [rows(SSD_INNER), rows(256), rows(256), rows(LANES),
                _const_spec((1, LANES)), _const_spec((LANES, SSD_INNER))]
    args = [xc, bm, cm, dt, alog_row, expand]
    scratch = [pltpu.VMEM((SSD_STATE, SSD_INNER), F32),
               pltpu.VMEM((t_len, t_len), F32),
               pltpu.VMEM((t_len, SSD_INNER // SSD_GROUPS), F32)]
    if combine:
        z, y_bwd, dskip_x, norm_w = combine_args
        in_specs += [rows(SSD_INNER), rows(SSD_INNER),
                     _const_spec((1, SSD_INNER)), _const_spec((1, SSD_INNER))]
        args += [z, y_bwd, dskip_x, norm_w]
        scratch.append(pltpu.VMEM((t_len, SSD_INNER), F32))
        out_dtype = BF16
    else:
        out_dtype = F32
    return pl.pallas_call(
        functools.partial(_ssd_body, reverse=reverse, combine=combine),
        grid=(nc,),
        in_specs=in_specs, out_specs=rows(SSD_INNER),
        out_shape=jax.ShapeDtypeStruct((n_rows, SSD_INNER), out_dtype),
        scratch_shapes=scratch,
        compiler_params=_params(), name="ssd_fwd" if combine else "ssd_bwd",
    )(*args)


def _attn_body(sink_ref, q_ref, kp_ref, km_ref, kn_ref, vp_ref, vm_ref, vn_ref, sg_ref, ys_ref,
               x_ref, w1_ref, w2_ref, o_ref, k_scr, v_scr, y_scr, s_scr, *, tq):
    i = pl.program_id(0)
    n = pl.num_programs(0)
    blk = ATTN_BLOCK
    nq = tq // blk
    for scr, p_ref, m_ref, n_ref in ((k_scr, kp_ref, km_ref, kn_ref), (v_scr, vp_ref, vm_ref, vn_ref)):
        scr[0:blk, :] = p_ref[...]
        scr[blk:blk + tq, :] = m_ref[...]
        scr[blk + tq:blk + tq + blk, :] = n_ref[...]

    row2 = lax.broadcasted_iota(jnp.int32, (2 * blk, blk), 0)
    qi = row2 & (blk - 1)
    kj = lax.broadcasted_iota(jnp.int32, (2 * blk, blk), 1)
    first_tile = row2[:, 0:1] < blk
    n_kv = ATTN_KV_HEADS

    def one_block(b, carry):
        r0 = pl.multiple_of(b * blk, blk)
        gb = i * nq + b
        bias_l = jnp.where((kj >= qi) & (gb > 0), 0.0, NEG)
        bias_r = jnp.where((kj <= qi) & (gb < n * nq - 1), 0.0, NEG)
        for hk in range(n_kv):
            q2 = jnp.concatenate(
                [q_ref[pl.ds(r0, blk), (2 * hk) * LANES:(2 * hk + 1) * LANES],
                 q_ref[pl.ds(r0, blk), (2 * hk + 1) * LANES:(2 * hk + 2) * LANES]], axis=0)
            for half in range(2):
                c = (2 * hk + half) * LANES
                kz = k_scr[pl.ds(r0, 3 * blk), c:c + LANES]
                s_scr[2 * hk + half] = lax.dot_general(
                    q2, kz, (((1,), (1,)), ((), ())), preferred_element_type=F32)
        for hk in range(n_kv):
            out = None
            for half in range(2):
                c = (2 * hk + half) * LANES
                s = s_scr[2 * hk + half]
                s_l = s[:, 0:blk] + bias_l
                s_m = s[:, blk:2 * blk]
                s_r = s[:, 2 * blk:3 * blk] + bias_r
                sink = jnp.where(first_tile, sink_ref[4 * hk + half], sink_ref[4 * hk + 2 + half])
                m = jnp.max(jnp.maximum(jnp.maximum(s_l, s_m), s_r), axis=-1, keepdims=True)
                m = jnp.maximum(m, sink)
                p_l = jnp.exp(s_l - m)
                p_m = jnp.exp(s_m - m)
                p_r = jnp.exp(s_r - m)
                den = jnp.sum(p_l + p_m + p_r, axis=-1, keepdims=True) + jnp.exp(sink - m)
                p = jnp.concatenate([p_l, p_m, p_r], axis=1).astype(BF16)
                o = _dot(p, v_scr[pl.ds(r0, 3 * blk), c:c + LANES]) * (1.0 / den)
                out = o if out is None else out + o
            for k in range(2):
                t = 2 * hk + k
                y_scr[pl.ds(r0, blk), t * LANES:(t + 1) * LANES] = (
                    out[k * blk:(k + 1) * blk, :]
                    * sg_ref[pl.ds(r0, blk), t * LANES:(t + 1) * LANES]).astype(BF16)
        return carry

    lax.fori_loop(0, nq, one_block, 0)
    o_ref[...] = x_ref[...] + _dot(ys_ref[...], w1_ref[...]) + _dot(y_scr[...], w2_ref[...])


def _attn_out(sink, q, kz, vz, sg, y_ssd, x, w1, w2):
    n_rows = x.shape[0]
    tq = ATTN_TILE
    blk = ATTN_BLOCK
    per = tq // blk
    nb = n_rows // blk
    rows = functools.partial(_row_spec, tq)
    prev = pl.BlockSpec((blk, D_MODEL), lambda i: (jnp.maximum(i * per - 1, 0), 0))
    nxt = pl.BlockSpec((blk, D_MODEL), lambda i: (jnp.minimum((i + 1) * per, nb - 1), 0))
    in_specs = [pl.BlockSpec(memory_space=pltpu.SMEM),
                rows(D_MODEL), prev, rows(D_MODEL), nxt, prev, rows(D_MODEL), nxt,
                rows(D_MODEL), rows(D_MODEL), rows(D_MODEL),
                _const_spec(w1.shape), _const_spec(w2.shape)]
    return pl.pallas_call(
        functools.partial(_attn_body, tq=tq),
        grid=(n_rows // tq,),
        in_specs=in_specs, out_specs=rows(D_MODEL),
        out_shape=jax.ShapeDtypeStruct((n_rows, D_MODEL), F32),
        scratch_shapes=[pltpu.VMEM((tq + 2 * blk, D_MODEL), BF16),
                        pltpu.VMEM((tq + 2 * blk, D_MODEL), BF16),
                        pltpu.VMEM((tq, D_MODEL), BF16),
                        pltpu.VMEM((2 * ATTN_KV_HEADS, 2 * blk, 3 * blk), F32)],
        compiler_params=_params(), name="attn_out",
    )(sink, q, kz, kz, kz, vz, vz, vz, sg, y_ssd, x, w1, w2)


def _odd_in_body(xp_ref, xm_ref, xn_ref, g_ref, wx_ref, wg_ref, cw_ref, cb_ref,
                 xc_ref, sg_ref, h_scr, u_scr, *, tm):
    _fill_normed(xp_ref, xm_ref, xn_ref, g_ref, h_scr, tm)
    h_ext = h_scr[...]
    hm = h_scr[HALO:HALO + tm, :]
    cstep = 256
    for c0 in range(0, LRU_WIDTH, cstep):
        outs = _proj_conv(h_ext, wx_ref, u_scr, cw_ref, cb_ref, tm, c0, cstep)
        for j, acc in enumerate(outs):
            xc_ref[:, c0 + j * LANES:c0 + (j + 1) * LANES] = acc
        sg_ref[:, c0:c0 + cstep] = _silu(_dot(hm, wg_ref[:, c0:c0 + cstep]))


def _odd_in(x, g, wx, wg, cw, cb):
    n_rows = x.shape[0]
    tm = IN_TILE_ODD
    rows = functools.partial(_row_spec, tm)
    in_specs = _halo_specs(tm, n_rows) + [
        _const_spec((1, D_MODEL)), _const_spec(wx.shape), _const_spec(wg.shape),
        _const_spec(cw.shape), _const_spec(cb.shape)]
    return pl.pallas_call(
        functools.partial(_odd_in_body, tm=tm),
        grid=(n_rows // tm,),
        in_specs=in_specs, out_specs=(rows(LRU_WIDTH), rows(LRU_WIDTH)),
        out_shape=(jax.ShapeDtypeStruct((n_rows, LRU_WIDTH), F32),
                   jax.ShapeDtypeStruct((n_rows, LRU_WIDTH), F32)),
        scratch_shapes=[pltpu.VMEM((tm + 2 * HALO, D_MODEL), BF16),
                        pltpu.VMEM((LRU_WIDTH // LANES, tm + 2 * HALO, LANES), F32)],
        compiler_params=_params(), name="odd_in",
    )(x, x, x, g, wx, wg, cw, cb)


def _lru_body(*refs, reverse, combine, final):
    if combine:
        if final:
            (xc_ref, w_ref, ba_ref, bx_ref, lam_ref, hb_ref, sg_ref, x_ref, wo_ref, fw_ref,
             o_ref, a_scr, u_scr, carry, n_scr, y_scr) = refs
        else:
            (xc_ref, w_ref, ba_ref, bx_ref, lam_ref, hb_ref, sg_ref, x_ref, wo_ref,
             o_ref, a_scr, u_scr, carry, n_scr, y_scr) = refs
    else:
        xc_ref, w_ref, ba_ref, bx_ref, lam_ref, o_ref, a_scr, u_scr, carry = refs
    tm, sub, pitch = LRU_TILE, LRU_SUB, LRU_PITCH
    vregs_per_sub = sub // SUBLANES
    grp_per_blk = LRU_BLOCK // LANES
    half_w = LRU_HALF_GROUPS * LANES

    @pl.when(pl.program_id(0) == 0)
    def _():
        carry[...] = jnp.zeros_like(carry)

    half_rate = (-0.5 * LRU_C) * jax.nn.softplus(-lam_ref[...])
    steps = list(range(sub))
    order = list(range(SUBLANES))
    if reverse:
        steps = steps[::-1]
        order = order[::-1]
    out = x_ref[...] if combine else None

    for hf in range(LRU_WIDTH // half_w):
        for b4 in range(half_w // LRU_BLOCK):
            c0 = hf * half_w + b4 * LRU_BLOCK
            xb = xc_ref[:, c0:c0 + LRU_BLOCK]
            ri = _dot(xb.astype(BF16), w_ref[c0 // LRU_BLOCK])
            hr = half_rate[:, c0:c0 + LRU_BLOCK]
            log_a = hr + hr * jnp.tanh(0.5 * (ri[:, :LRU_BLOCK] + ba_ref[:, c0:c0 + LRU_BLOCK]))
            hx = 0.5 * xb
            gx = hx + hx * jnp.tanh(0.5 * (ri[:, LRU_BLOCK:] + bx_ref[:, c0:c0 + LRU_BLOCK]))
            a = jnp.exp(log_a)
            th = jnp.tanh(log_a)
            num = -2.0 * th
            root = jnp.where(num > 0.0, num * lax.rsqrt(num), 0.0) * lax.rsqrt(1.0 - th)
            u = root * gx
            for half in range(grp_per_blk):
                g = b4 * grp_per_blk + half
                ls = slice(half * LANES, (half + 1) * LANES)
                for k in range(tm // SUBLANES):
                    s = k // vregs_per_sub
                    jj = (k % vregs_per_sub) * SUBLANES
                    dst = pl.ds(jj * SUBLANES + s, SUBLANES, stride=SUBLANES)
                    a_scr[g, dst, :] = a[k * SUBLANES:(k + 1) * SUBLANES, ls]
                    u_scr[g, dst, :] = u[k * SUBLANES:(k + 1) * SUBLANES, ls]

        if combine and hf > 0:
            ps = slice((hf - 1) * half_w, hf * half_w)
            out = out + _dot(y_scr[:, ps], wo_ref[ps, :])

        def rows_t(j):
            return slice(j * SUBLANES, (j + 1) * SUBLANES)

        h = u_scr[:, rows_t(steps[0]), :]
        p = a_scr[:, rows_t(steps[0]), :]
        for j in steps[1:]:
            aj = a_scr[:, rows_t(j), :]
            h = aj * h + u_scr[:, rows_t(j), :]
            p = aj * p
            u_scr[:, rows_t(j), :] = h
            a_scr[:, rows_t(j), :] = p

        gs = slice(hf * LRU_HALF_GROUPS, (hf + 1) * LRU_HALF_GROUPS)
        c = carry[gs]
        carry_in = [None] * SUBLANES
        for s in order:
            carry_in[s] = c
            c = p[:, s:s + 1, :] * c + h[:, s:s + 1, :]
        carry[gs] = c
        cmat = jnp.concatenate(carry_in, axis=1)

        for j in range(sub):
            hj = u_scr[:, rows_t(j), :] + a_scr[:, rows_t(j), :] * cmat
            for g in range(LRU_HALF_GROUPS):
                cs_ = slice(hf * half_w + g * LANES, hf * half_w + (g + 1) * LANES)
                if combine:
                    n_scr[g, pl.ds(j, SUBLANES, stride=pitch), :] = hj[g] + hb_ref[rows_t(j), cs_]
                else:
                    o_ref[rows_t(j), cs_] = hj[g]

        if combine:
            for s in range(SUBLANES):
                rs = slice(s * sub, (s + 1) * sub)
                for g in range(LRU_HALF_GROUPS):
                    cs_ = slice(hf * half_w + g * LANES, hf * half_w + (g + 1) * LANES)
                    y_scr[rs, cs_] = (n_scr[g, s * pitch:s * pitch + sub, :] * sg_ref[rs, cs_]).astype(BF16)

    if combine:
        hs = slice(LRU_WIDTH - half_w, LRU_WIDTH)
        out = out + _dot(y_scr[:, hs], wo_ref[hs, :])
        if final:
            out = _rms(out, fw_ref[...])
        o_ref[...] = out


def _lru(xc, w_gates, b_a, b_x, lam, reverse, combine_args=None, final_w=None):
    n_rows = xc.shape[0]
    tm = LRU_TILE
    n = n_rows // tm
    rows = functools.partial(_row_spec, tm, rev_n=n if reverse else None)
    combine = combine_args is not None
    final = final_w is not None
    in_specs = [rows(LRU_WIDTH), _const_spec(w_gates.shape), _const_spec((1, LRU_WIDTH)),
                _const_spec((1, LRU_WIDTH)), _const_spec((1, LRU_WIDTH))]
    args = [xc, w_gates, b_a, b_x, lam]
    n_grp = LRU_WIDTH // LANES
    scratch = [pltpu.VMEM((LRU_HALF_GROUPS, tm, LANES), F32),
               pltpu.VMEM((LRU_HALF_GROUPS, tm, LANES), F32),
               pltpu.VMEM((n_grp, 1, LANES), F32)]
    if combine:
        h_bwd, sg, x, w_out = combine_args
        in_specs += [rows(LRU_WIDTH), rows(LRU_WIDTH), rows(D_MODEL), _const_spec(w_out.shape)]
        args += [h_bwd, sg, x, w_out]
        if final:
            in_specs.append(_const_spec((1, D_MODEL)))
            args.append(final_w)
        scratch.append(pltpu.VMEM((LRU_HALF_GROUPS, SUBLANES * LRU_PITCH, LANES), F32))
        scratch.append(pltpu.VMEM((tm, LRU_WIDTH), BF16))
        out_w = D_MODEL
    else:
        out_w = LRU_WIDTH
    return pl.pallas_call(
        functools.partial(_lru_body, reverse=reverse, combine=combine, final=final),
        grid=(n,),
        in_specs=in_specs, out_specs=rows(out_w),
        out_shape=jax.ShapeDtypeStruct((n_rows, out_w), F32),
        scratch_shapes=scratch,
        compiler_params=_params(), name="lru_fwd" if combine else "lru_bwd",
    )(*args)


def _rope_tables(n_rows):
    inv = ROPE_THETA ** (-jnp.arange(0, ROPE_DIM, 2, dtype=F32) / ROPE_DIM)
    ang = jnp.arange(n_rows, dtype=F32)[:, None] * inv[None, :]
    cos, sin = jnp.cos(ang), jnp.sin(ang)
    half = ROPE_DIM // 2
    ones = jnp.ones((n_rows, ATTN_HEAD_DIM - ROPE_DIM), F32)
    zeros = jnp.zeros((n_rows, ATTN_HEAD_DIM - ROPE_DIM), F32)
    zero_h = jnp.zeros((n_rows, half), F32)
    reps = LANES // ATTN_HEAD_DIM
    cos_t = jnp.tile(jnp.concatenate([cos, cos, ones], axis=1), (1, reps))
    sin_a = jnp.tile(jnp.concatenate([-sin, zero_h, zeros], axis=1), (1, reps))
    sin_b = jnp.tile(jnp.concatenate([zero_h, sin, zeros], axis=1), (1, reps))
    return cos_t, sin_a, sin_b


def _pad_lanes(v):
    return jnp.pad(v.astype(F32), (0, LANES - v.shape[0]))[None, :]


def _even_layer(x, g_norm, w_in, conv_w, conv_b, dt_bias, a_log, d_skip, ssd_norm_w, sink, w_out, rope):
    sizes = (SSD_INNER, SSD_CONV_CH, 2 * SSD_HEADS, D_MODEL, 256, 256, D_MODEL)
    cuts = [0]
    for s in sizes:
        cuts.append(cuts[-1] + s)
    wz, wx, wdt, wq, wk, wv, wg = (w_in[:, cuts[k]:cuts[k + 1]].astype(BF16) for k in range(7))
    wdt = jnp.pad(wdt, ((0, 0), (0, LANES - 2 * SSD_HEADS)))
    dtb = _pad_lanes(dt_bias.reshape(-1))
    z, xc, bm, cm, dt, q, kz, vz, sg = _even_in(
        x, g_norm[None, :], wz, wx, wdt, wq, wk, wv, wg, conv_w, conv_b[None, :], dtb, *rope)

    lane = jnp.arange(LANES)[:, None]
    head = (jnp.arange(SSD_INNER) // SSD_HEAD_DIM)[None, :]
    alog_row = _pad_lanes(a_log.reshape(-1))
    y_bwd = _ssd(xc, bm, cm, dt, alog_row, (lane == head + SSD_HEADS).astype(BF16), reverse=True)
    dskip_x = jnp.repeat(d_skip.astype(F32), SSD_HEAD_DIM)[None, :]
    y_ssd = _ssd(xc, bm, cm, dt, alog_row, (lane == head).astype(BF16), reverse=False,
                 combine_args=(z, y_bwd, dskip_x, ssd_norm_w[None, :]))

    w_out_b = w_out.astype(BF16)
    return _attn_out(sink.astype(F32), q, kz, vz, sg, y_ssd, x, w_out_b[:SSD_INNER], w_out_b[SSD_INNER:])


def _odd_layer(x, g_norm, w_in, conv_w, conv_b, w_a, b_a, w_x, b_x, lam, w_out, final_w):
    w_in_b = w_in.astype(BF16)
    xc, sg = _odd_in(x, g_norm[None, :], w_in_b[:, :LRU_WIDTH], w_in_b[:, LRU_WIDTH:],
                     conv_w, conv_b[None, :])
    w_gates = jnp.concatenate([w_a, w_x], axis=-1).astype(BF16)
    h_bwd = _lru(xc, w_gates[1], b_a[1][None, :], b_x[1][None, :], lam[1][None, :], reverse=True)
    return _lru(xc, w_gates[0], b_a[0][None, :], b_x[0][None, :], lam[0][None, :], reverse=False,
                combine_args=(h_bwd, sg, x, w_out.astype(BF16)),
                final_w=None if final_w is None else final_w[None, :])


def kernel(x, norm_w, final_norm_w, ev_w_in, ev_conv_w, ev_conv_b, ev_dt_bias, ev_a_log, ev_d_skip, ev_ssd_norm_w, ev_sink, ev_w_out, od_w_in, od_conv_w, od_conv_b, od_w_a, od_b_a, od_w_x, od_b_x, od_lambda, od_w_out):
    batch, n_rows, _ = x.shape
    depth = norm_w.shape[0]
    assert batch == 1 and depth % 2 == 0
    rope = _rope_tables(n_rows)
    h = x[0]
    for layer in range(depth):
        j = layer // 2
        if layer % 2 == 0:
            h = _even_layer(h, norm_w[layer], ev_w_in[j], ev_conv_w[j], ev_conv_b[j], ev_dt_bias[j],
                            ev_a_log[j], ev_d_skip[j], ev_ssd_norm_w[j], ev_sink[j], ev_w_out[j], rope)
        else:
            h = _odd_layer(h, norm_w[layer], od_w_in[j], od_conv_w[j], od_conv_b[j], od_w_a[j],
                           od_b_a[j], od_w_x[j], od_b_x[j], od_lambda[j], od_w_out[j],
                           final_norm_w if layer == depth - 1 else None)
    return h[None]
```

```python
import functools

import jax
import jax.numpy as jnp
from jax import lax
from jax.experimental import pallas as pl
from jax.experimental.pallas import tpu as pltpu

F32 = jnp.float32
BF16 = jnp.bfloat16

D_MODEL = 1024
EPS = 1e-6
LANES = 128
SUBLANES = 8
HALO = 16
CONV_W = 4
CONV_LEFT = 2

SSD_HEADS = 16
SSD_HEAD_DIM = 64
SSD_STATE = 128
SSD_GROUPS = 2
SSD_CHUNK = 256
SSD_INNER = 1024
SSD_CONV_CH = 1536

ATTN_HEADS = 16
ATTN_KV_HEADS = 4
ATTN_HEAD_DIM = 64
ATTN_BLOCK = 128
ROPE_THETA = 500000.0
ROPE_DIM = 16

LRU_WIDTH = 2048
LRU_BLOCK = 256
LRU_BLOCKS = 8
LRU_C = 8.0

NEG = -1e30
VMEM_LIMIT = 56 * 1024 * 1024

IN_TILE_EVEN = 512
IN_TILE_ODD = 512
ATTN_TILE = 512
LRU_TILE = 256
LRU_SUB = LRU_TILE // SUBLANES
LRU_PITCH = LRU_SUB + SUBLANES
LRU_HALF_GROUPS = 8


def _dot(a, b):
    return jnp.dot(a, b, preferred_element_type=F32)


def _split3(v):
    hi = v.astype(BF16)
    r1 = v - hi.astype(F32)
    mid = r1.astype(BF16)
    lo = (r1 - mid.astype(F32)).astype(BF16)
    return hi, mid, lo


def _dot3(*ops, rhs_split=True):
    if rhs_split:
        m, parts = ops[0], ops[1:]
        return _dot(m, parts[0]) + _dot(m, parts[1]) + _dot(m, parts[2])
    parts, m = ops[:3], ops[3]
    return _dot(parts[0], m) + _dot(parts[1], m) + _dot(parts[2], m)


def _rms(x, g):
    ms = jnp.mean(x * x, axis=-1, keepdims=True)
    return x * lax.rsqrt(ms + EPS) * g


def _sigmoid(x):
    return 0.5 * jnp.tanh(0.5 * x) + 0.5


def _silu(x):
    h = 0.5 * x
    return h + h * jnp.tanh(h)


def _params(n_axes=1):
    return pltpu.CompilerParams(dimension_semantics=("arbitrary",) * n_axes,
                                vmem_limit_bytes=VMEM_LIMIT)


def _const_spec(shape):
    nd = len(shape)
    return pl.BlockSpec(shape, lambda i: (0,) * nd, pipeline_mode=pl.Buffered(1))


def _row_spec(tm, width, rev_n=None):
    if rev_n is None:
        return pl.BlockSpec((tm, width), lambda i: (i, 0))
    return pl.BlockSpec((tm, width), lambda i: (rev_n - 1 - i, 0))


def _halo_specs(tm, n_rows):
    hb = tm // HALO
    nh = n_rows // HALO
    prev = pl.BlockSpec((HALO, D_MODEL), lambda i: (jnp.maximum(i * hb - 1, 0), 0))
    main = pl.BlockSpec((tm, D_MODEL), lambda i: (i, 0))
    nxt = pl.BlockSpec((HALO, D_MODEL), lambda i: (jnp.minimum((i + 1) * hb, nh - 1), 0))
    return [prev, main, nxt]


def _fill_normed(xp_ref, xm_ref, xn_ref, g_ref, h_scr, tm):
    i = pl.program_id(0)
    n = pl.num_programs(0)
    g = g_ref[...]
    hp = _rms(xp_ref[...], g)
    hn = _rms(xn_ref[...], g)
    h_scr[0:HALO, :] = jnp.where(i > 0, hp, 0.0).astype(BF16)
    h_scr[HALO:HALO + tm, :] = _rms(xm_ref[...], g).astype(BF16)
    h_scr[HALO + tm:HALO + tm + HALO, :] = jnp.where(i < n - 1, hn, 0.0).astype(BF16)


def _proj_conv(h_ext, w_ref, u_scr, cw_ref, cb_ref, tm, c0, width):
    u = _dot(h_ext, w_ref[:, c0:c0 + width])
    base = HALO - CONV_LEFT
    outs = []
    for j in range(width // LANES):
        slab = c0 // LANES + j
        u_scr[slab] = u[:, j * LANES:(j + 1) * LANES]
        cs_ = slice(c0 + j * LANES, c0 + (j + 1) * LANES)
        acc = cb_ref[:, cs_] + cw_ref[0:1, cs_] * u_scr[slab, base:base + tm, :]
        for k in range(1, CONV_W):
            acc = acc + cw_ref[k:k + 1, cs_] * u_scr[slab, base + k:base + k + tm, :]
        outs.append(acc)
    return outs


def _even_in_body(xp_ref, xm_ref, xn_ref, g_ref, wz_ref, wx_ref, wdt_ref, wq_ref, wk_ref, wv_ref,
                  wg_ref, cw_ref, cb_ref, dtb_ref, cos_ref, sa_ref, sb_ref,
                  z_ref, xc_ref, b_ref, c_ref, dt_ref, q_ref, kz_ref, vz_ref, sg_ref,
                  h_scr, u_scr, *, tm):
    _fill_normed(xp_ref, xm_ref, xn_ref, g_ref, h_scr, tm)
    h_ext = h_scr[...]
    hm = h_scr[HALO:HALO + tm, :]
    cstep = 256
    cos_t = cos_ref[...]
    sin_a = sa_ref[...]
    sin_b = sb_ref[...]
    low = lax.broadcasted_iota(jnp.int32, (tm, LANES), 1) < ATTN_HEAD_DIM

    def rope(t):
        return (t * cos_t + pltpu.roll(t, LANES - ROPE_DIM // 2, 1) * sin_a
                + pltpu.roll(t, ROPE_DIM // 2, 1) * sin_b)

    def do_xbc(c0):
        outs = _proj_conv(h_ext, wx_ref, u_scr, cw_ref, cb_ref, tm, c0, cstep)
        for j, acc in enumerate(outs):
            y = _silu(acc)
            c = c0 + j * LANES
            if c < SSD_INNER:
                xc_ref[:, c:c + LANES] = y
            elif c < SSD_INNER + 256:
                b_ref[:, c - SSD_INNER:c - SSD_INNER + LANES] = y
            else:
                c_ref[:, c - SSD_INNER - 256:c - SSD_INNER - 256 + LANES] = y

    def do_z(c0):
        z_ref[:, c0:c0 + cstep] = _dot(hm, wz_ref[:, c0:c0 + cstep])

    def do_gate(c0):
        sg_ref[:, c0:c0 + cstep] = _silu(_dot(hm, wg_ref[:, c0:c0 + cstep]))

    def do_q(c0):
        qf = _dot(hm, wq_ref[:, c0:c0 + cstep]) * (ATTN_HEAD_DIM ** -0.5)
        for j in range(cstep // LANES):
            q_ref[:, c0 + j * LANES:c0 + (j + 1) * LANES] = rope(qf[:, j * LANES:(j + 1) * LANES]).astype(BF16)

    def do_kv(w_ref, dst, rotary):
        f = _dot(hm, w_ref[...])
        for j in range(2):
            a = f[:, j * LANES:(j + 1) * LANES]
            if rotary:
                a = rope(a)
            r = pltpu.roll(a, ATTN_HEAD_DIM, 1)
            c = 4 * j * LANES
            dst[:, c:c + LANES] = jnp.where(low, a, 0.0).astype(BF16)
            dst[:, c + LANES:c + 2 * LANES] = jnp.where(low, 0.0, r).astype(BF16)
            dst[:, c + 2 * LANES:c + 3 * LANES] = jnp.where(low, r, 0.0).astype(BF16)
            dst[:, c + 3 * LANES:c + 4 * LANES] = jnp.where(low, 0.0, a).astype(BF16)

    for n in range(4):
        do_xbc(n * cstep)
        do_z(n * cstep)
        do_q(n * cstep)
        do_gate(n * cstep)
        if n == 0:
            do_xbc(4 * cstep)
        elif n == 1:
            do_xbc(5 * cstep)
        elif n == 2:
            do_kv(wk_ref, kz_ref, True)
        else:
            do_kv(wv_ref, vz_ref, False)
    dt_ref[...] = jax.nn.softplus(_dot(hm, wdt_ref[...]) + dtb_ref[...])


def _even_in(x, g, wz, wx, wdt, wq, wk, wv, wg, cw, cb, dtb, cos_t, sin_a, sin_b):
    n_rows = x.shape[0]
    tm = IN_TILE_EVEN
    rows = functools.partial(_row_spec, tm)
    in_specs = _halo_specs(tm, n_rows) + [
        _const_spec((1, D_MODEL)),
        _const_spec(wz.shape), _const_spec(wx.shape), _const_spec(wdt.shape), _const_spec(wq.shape),
        _const_spec(wk.shape), _const_spec(wv.shape), _const_spec(wg.shape),
        _const_spec(cw.shape), _const_spec(cb.shape), _const_spec(dtb.shape),
        rows(LANES), rows(LANES), rows(LANES),
    ]
    out_shape = (
        jax.ShapeDtypeStruct((n_rows, SSD_INNER), F32),
        jax.ShapeDtypeStruct((n_rows, SSD_INNER), F32),
        jax.ShapeDtypeStruct((n_rows, 256), F32),
        jax.ShapeDtypeStruct((n_rows, 256), F32),
        jax.ShapeDtypeStruct((n_rows, LANES), F32),
        jax.ShapeDtypeStruct((n_rows, D_MODEL), BF16),
        jax.ShapeDtypeStruct((n_rows, D_MODEL), BF16),
        jax.ShapeDtypeStruct((n_rows, D_MODEL), BF16),
        jax.ShapeDtypeStruct((n_rows, D_MODEL), F32),
    )
    out_specs = (rows(SSD_INNER), rows(SSD_INNER), rows(256), rows(256), rows(LANES),
                 rows(D_MODEL), rows(D_MODEL), rows(D_MODEL), rows(D_MODEL))
    return pl.pallas_call(
        functools.partial(_even_in_body, tm=tm),
        grid=(n_rows // tm,),
        in_specs=in_specs, out_specs=out_specs, out_shape=out_shape,
        scratch_shapes=[pltpu.VMEM((tm + 2 * HALO, D_MODEL), BF16),
                        pltpu.VMEM((SSD_CONV_CH // LANES, tm + 2 * HALO, LANES), F32)],
        compiler_params=_params(), name="even_in",
    )(x, x, x, g, wz, wx, wdt, wq, wk, wv, wg, cw, cb, dtb, cos_t, sin_a, sin_b)


def _ssd_body(*refs, reverse, combine):
    if combine:
        (xc_ref, b_ref, c_ref, dt_ref, alog_ref, e_ref, z_ref, yb_ref, dsk_ref, nw_ref,
         y_ref, state, cb_scr, yoff_scr, ytmp) = refs
    else:
        xc_ref, b_ref, c_ref, dt_ref, alog_ref, e_ref, y_ref, state, cb_scr, yoff_scr = refs
        ytmp = y_ref
    t_len = SSD_CHUNK
    off = SSD_HEADS if reverse else 0

    @pl.when(pl.program_id(0) == 0)
    def _():
        state[...] = jnp.zeros_like(state)

    row = lax.broadcasted_iota(jnp.int32, (t_len, t_len), 0)
    col = lax.broadcasted_iota(jnp.int32, (t_len, t_len), 1)
    mask = (col >= row) if reverse else (col <= row)
    tri = jnp.where(mask, 1.0, 0.0).astype(BF16)

    lane = lax.broadcasted_iota(jnp.int32, (1, LANES), 1)
    in_dir = (lane >= off) & (lane < off + SSD_HEADS)
    a_rate = jnp.where(in_dir, -jnp.exp(alog_ref[...]), 0.0)

    dt = dt_ref[...]
    cs = _dot3(tri, *_split3(dt * a_rate))
    log_dt = jnp.log(dt)
    cs_t = (cs - log_dt).T
    last = 0 if reverse else t_len - 1
    cs_last = cs[last:last + 1, :]
    w_hi, w_mid, _ = _split3(jnp.exp(cs_last - cs + log_dt))
    expand = e_ref[...]
    wdt_x = _dot(w_hi, expand) + _dot(w_mid, expand)

    b_f = b_ref[...]
    c_b = c_ref[...].astype(BF16)
    b_b = b_f.astype(BF16)
    b_t = b_f.T.astype(BF16)
    low = lax.broadcasted_iota(jnp.int32, (t_len, LANES), 1) < SSD_HEAD_DIM

    hb = t_len // 2
    key_blocks = ((0, 1), (1,)) if reverse else ((0,), (0, 1))
    diag_mask = mask[0:hb, 0:hb]

    gw = SSD_INNER // SSD_GROUPS
    for g in range(SSD_GROUPS):
        c_g = c_b[:, g * SSD_STATE:(g + 1) * SSD_STATE]
        b_g = b_b[:, g * SSD_STATE:(g + 1) * SSD_STATE]
        cb_scr[...] = lax.dot_general(c_g, b_g, (((1,), (1,)), ((), ())), preferred_element_type=F32)
        s_g = state[:, g * gw:(g + 1) * gw]
        yoff_scr[...] = _dot(c_g, s_g.astype(BF16))
        edge_decay = []
        for tt in range(gw // LANES):
            t = g * (gw // LANES) + tt
            ts = slice(t * LANES, (t + 1) * LANES)
            xt = xc_ref[:, ts]
            x_halves = (jnp.where(low, xt, 0.0).astype(BF16), jnp.where(low, 0.0, xt).astype(BF16))
            cols = [jnp.broadcast_to(cs[:, off + 2 * t + half:off + 2 * t + half + 1], (t_len, LANES))
                    for half in range(2)]
            e_tile = jnp.where(low, jnp.exp(cols[0]), jnp.exp(cols[1]))
            for rb in range(2):
                rs = slice(rb * hb, (rb + 1) * hb)
                acc = yoff_scr[rs, tt * LANES:(tt + 1) * LANES] * e_tile[rs, :]
                for half in range(2):
                    ln = off + 2 * t + half
                    pieces = []
                    for kb in key_blocks[rb]:
                        ks = slice(kb * hb, (kb + 1) * hb)
                        seg = cols[half][rs, :] - cs_t[ln:ln + 1, ks]
                        if kb == rb:
                            seg = jnp.where(diag_mask, seg, NEG)
                        pieces.append((cb_scr[rs, ks] * jnp.exp(seg)).astype(BF16))
                    m = pieces[0] if len(pieces) == 1 else jnp.concatenate(pieces, axis=1)
                    k0, k1 = key_blocks[rb][0] * hb, (key_blocks[rb][-1] + 1) * hb
                    acc = acc + _dot(m, x_halves[half][k0:k1, :])
                ytmp[rs, ts] = acc
            edge_decay.append(e_tile[last:last + 1, :])
        gs = slice(g * gw, (g + 1) * gw)
        xw_b = (xc_ref[:, gs] * wdt_x[:, gs]).astype(BF16)
        state[:, gs] = (jnp.concatenate(edge_decay, axis=1) * s_g
                        + _dot(b_t[g * SSD_STATE:(g + 1) * SSD_STATE, :], xw_b))

    if combine:
        y = ytmp[...] + yb_ref[...] + xc_ref[...] * dsk_ref[...]
        y = y * _silu(z_ref[...])
        for g in range(SSD_GROUPS):
            yg = y[:, g * gw:(g + 1) * gw]
            ms = jnp.mean(yg * yg, axis=-1, keepdims=True)
            y_ref[:, g * gw:(g + 1) * gw] = (
                yg * lax.rsqrt(ms + EPS) * nw_ref[:, g * gw:(g + 1) * gw]).astype(BF16)


def _ssd(xc, bm, cm, dt, alog_row, expand, reverse, combine_args=None):
    n_rows = xc.shape[0]
    t_len = SSD_CHUNK
    nc = n_rows // t_len
    rows = functools.partial(_row_spec, t_len, rev_n=nc if reverse else None)
    combine = combine_args is not None
    in_specs = [rows(SSD_INNER), rows(256), rows(256), rows(LANES),
                _const_spec((1, LANES)), _const_spec((LANES, SSD_INNER))]
    args = [xc, bm, cm, dt, alog_row, expand]
    scratch = [pltpu.VMEM((SSD_STATE, SSD_INNER), F32),
               pltpu.VMEM((t_len, t_len), F32),
               pltpu.VMEM((t_len, SSD_INNER // SSD_GROUPS), F32)]
    if combine:
        z, y_bwd, dskip_x, norm_w = combine_args
        in_specs += [rows(SSD_INNER), rows(SSD_INNER),
                     _const_spec((1, SSD_INNER)), _const_spec((1, SSD_INNER))]
        args += [z, y_bwd, dskip_x, norm_w]
        scratch.append(pltpu.VMEM((t_len, SSD_INNER), F32))
        out_dtype = BF16
    else:
        out_dtype = F32
    return pl.pallas_call(
        functools.partial(_ssd_body, reverse=reverse, combine=combine),
        grid=(nc,),
        in_specs=in_specs, out_specs=rows(SSD_INNER),
        out_shape=jax.ShapeDtypeStruct((n_rows, SSD_INNER), out_dtype),
        scratch_shapes=scratch,
        compiler_params=_params(), name="ssd_fwd" if combine else "ssd_bwd",
    )(*args)


def _attn_body(sink_ref, q_ref, kp_ref, km_ref, kn_ref, vp_ref, vm_ref, vn_ref, sg_ref, ys_ref,
               x_ref, w1_ref, w2_ref, o_ref, k_scr, v_scr, y_scr, yprev_scr, s_scr, *, tq):
    i = pl.program_id(0)
    n = pl.num_programs(0) - 1
    tile = jnp.minimum(i, n - 1)
    blk = ATTN_BLOCK
    nq = tq // blk
    wc = D_MODEL // nq

    @pl.when(i == 0)
    def _():
        y_scr[...] = jnp.zeros_like(y_scr)

    yprev_scr[...] = y_scr[...]
    for scr, p_ref, m_ref, n_ref in ((k_scr, kp_ref, km_ref, kn_ref), (v_scr, vp_ref, vm_ref, vn_ref)):
        scr[0:blk, :] = p_ref[...]
        scr[blk:blk + tq, :] = m_ref[...]
        scr[blk + tq:blk + tq + blk, :] = n_ref[...]

    row2 = lax.broadcasted_iota(jnp.int32, (2 * blk, blk), 0)
    qi = row2 & (blk - 1)
    kj = lax.broadcasted_iota(jnp.int32, (2 * blk, blk), 1)
    first_tile = row2[:, 0:1] < blk
    n_kv = ATTN_KV_HEADS

    for b in range(nq):
        r0 = b * blk
        gb = tile * nq + b
        bias_l = jnp.where((kj >= qi) & (gb > 0), 0.0, NEG)
        bias_r = jnp.where((kj <= qi) & (gb < n * nq - 1), 0.0, NEG)
        for hk in range(n_kv):
            q2 = jnp.concatenate(
                [q_ref[pl.ds(r0, blk), (2 * hk) * LANES:(2 * hk + 1) * LANES],
                 q_ref[pl.ds(r0, blk), (2 * hk + 1) * LANES:(2 * hk + 2) * LANES]], axis=0)
            kz = jnp.concatenate(
                [k_scr[pl.ds(r0, 3 * blk), (2 * hk + half) * LANES:(2 * hk + half + 1) * LANES]
                 for half in range(2)], axis=0)
            s_scr[hk] = lax.dot_general(q2, kz, (((1,), (1,)), ((), ())), preferred_element_type=F32)
        oc = slice(b * wc, (b + 1) * wc)
        o_ref[:, oc] = (x_ref[:, oc] + _dot(ys_ref[...], w1_ref[:, oc])
                        + _dot(yprev_scr[...], w2_ref[:, oc]))
        for hk in range(n_kv):
            out = None
            for half in range(2):
                c = (2 * hk + half) * LANES
                k0 = half * 3 * blk
                s_l = s_scr[hk, :, k0:k0 + blk] + bias_l
                s_m = s_scr[hk, :, k0 + blk:k0 + 2 * blk]
                s_r = s_scr[hk, :, k0 + 2 * blk:k0 + 3 * blk] + bias_r
                sink = jnp.where(first_tile, sink_ref[4 * hk + half], sink_ref[4 * hk + 2 + half])
                m = jnp.max(jnp.maximum(jnp.maximum(s_l, s_m), s_r), axis=-1, keepdims=True)
                m = jnp.maximum(m, sink)
                p_l = jnp.exp(s_l - m)
                p_m = jnp.exp(s_m - m)
                p_r = jnp.exp(s_r - m)
                den = jnp.sum(p_l + p_m + p_r, axis=-1, keepdims=True) + jnp.exp(sink - m)
                p = jnp.concatenate([p_l, p_m, p_r], axis=1).astype(BF16)
                o = _dot(p, v_scr[pl.ds(r0, 3 * blk), c:c + LANES]) * (1.0 / den)
                out = o if out is None else out + o
            for k in range(2):
                t = 2 * hk + k
                y_scr[pl.ds(r0, blk), t * LANES:(t + 1) * LANES] = (
                    out[k * blk:(k + 1) * blk, :]
                    * sg_ref[pl.ds(r0, blk), t * LANES:(t + 1) * LANES]).astype(BF16)


def _attn_out(sink, q, kz, vz, sg, y_ssd, x, w1, w2):
    n_rows = x.shape[0]
    tq = ATTN_TILE
    blk = ATTN_BLOCK
    per = tq // blk
    nb = n_rows // blk
    n = n_rows // tq

    def cur(i):
        return jnp.minimum(i, n - 1)

    def lag(i):
        return jnp.maximum(i - 1, 0)

    rows_cur = pl.BlockSpec((tq, D_MODEL), lambda i: (cur(i), 0))
    rows_lag = pl.BlockSpec((tq, D_MODEL), lambda i: (lag(i), 0))
    prev = pl.BlockSpec((blk, D_MODEL), lambda i: (jnp.maximum(cur(i) * per - 1, 0), 0))
    nxt = pl.BlockSpec((blk, D_MODEL), lambda i: (jnp.minimum((cur(i) + 1) * per, nb - 1), 0))
    in_specs = [pl.BlockSpec(memory_space=pltpu.SMEM),
                rows_cur, prev, rows_cur, nxt, prev, rows_cur, nxt,
                rows_cur, rows_lag, rows_lag,
                _const_spec(w1.shape), _const_spec(w2.shape)]
    return pl.pallas_call(
        functools.partial(_attn_body, tq=tq),
        grid=(n + 1,),
        in_specs=in_specs, out_specs=rows_lag,
        out_shape=jax.ShapeDtypeStruct((n_rows, D_MODEL), F32),
        scratch_shapes=[pltpu.VMEM((tq + 2 * blk, D_MODEL), BF16),
                        pltpu.VMEM((tq + 2 * blk, D_MODEL), BF16),
                        pltpu.VMEM((tq, D_MODEL), BF16),
                        pltpu.VMEM((tq, D_MODEL), BF16),
                        pltpu.VMEM((ATTN_KV_HEADS, 2 * blk, 6 * blk), F32)],
        compiler_params=_params(), name="attn_out",
    )(sink, q, kz, kz, kz, vz, vz, vz, sg, y_ssd, x, w1, w2)


def _odd_in_body(xp_ref, xm_ref, xn_ref, g_ref, wx_ref, wg_ref, cw_ref, cb_ref,
                 xc_ref, sg_ref, h_scr, u_scr, *, tm):
    _fill_normed(xp_ref, xm_ref, xn_ref, g_ref, h_scr, tm)
    h_ext = h_scr[...]
    hm = h_scr[HALO:HALO + tm, :]
    cstep = 256
    for c0 in range(0, LRU_WIDTH, cstep):
        outs = _proj_conv(h_ext, wx_ref, u_scr, cw_ref, cb_ref, tm, c0, cstep)
        for j, acc in enumerate(outs):
            xc_ref[:, c0 + j * LANES:c0 + (j + 1) * LANES] = acc
        sg_ref[:, c0:c0 + cstep] = _silu(_dot(hm, wg_ref[:, c0:c0 + cstep]))


def _odd_in(x, g, wx, wg, cw, cb):
    n_rows = x.shape[0]
    tm = IN_TILE_ODD
    rows = functools.partial(_row_spec, tm)
    in_specs = _halo_specs(tm, n_rows) + [
        _const_spec((1, D_MODEL)), _const_spec(wx.shape), _const_spec(wg.shape),
        _const_spec(cw.shape), _const_spec(cb.shape)]
    return pl.pallas_call(
        functools.partial(_odd_in_body, tm=tm),
        grid=(n_rows // tm,),
        in_specs=in_specs, out_specs=(rows(LRU_WIDTH), rows(LRU_WIDTH)),
        out_shape=(jax.ShapeDtypeStruct((n_rows, LRU_WIDTH), F32),
                   jax.ShapeDtypeStruct((n_rows, LRU_WIDTH), F32)),
        scratch_shapes=[pltpu.VMEM((tm + 2 * HALO, D_MODEL), BF16),
                        pltpu.VMEM((LRU_WIDTH // LANES, tm + 2 * HALO, LANES), F32)],
        compiler_params=_params(), name="odd_in",
    )(x, x, x, g, wx, wg, cw, cb)


def _lru_body(*refs, reverse, combine, final):
    if combine:
        if final:
            (xc_ref, w_ref, ba_ref, bx_ref, lam_ref, hb_ref, sg_ref, x_ref, wo_ref, fw_ref,
             o_ref, a_scr, u_scr, carry, n_scr, y_scr) = refs
        else:
            (xc_ref, w_ref, ba_ref, bx_ref, lam_ref, hb_ref, sg_ref, x_ref, wo_ref,
             o_ref, a_scr, u_scr, carry, n_scr, y_scr) = refs
    else:
        xc_ref, w_ref, ba_ref, bx_ref, lam_ref, o_ref, a_scr, u_scr, carry = refs
    tm, sub, pitch = LRU_TILE, LRU_SUB, LRU_PITCH
    vregs_per_sub = sub // SUBLANES
    grp_per_blk = LRU_BLOCK // LANES
    half_w = LRU_HALF_GROUPS * LANES

    @pl.when(pl.program_id(0) == 0)
    def _():
        carry[...] = jnp.zeros_like(carry)

    half_rate = (-0.5 * LRU_C) * jax.nn.softplus(-lam_ref[...])
    steps = list(range(sub))
    order = list(range(SUBLANES))
    if reverse:
        steps = steps[::-1]
        order = order[::-1]

    for blk in range(LRU_BLOCKS):
        c0 = blk * LRU_BLOCK
        xb = xc_ref[:, c0:c0 + LRU_BLOCK]
        ri = _dot(xb.astype(BF16), w_ref[blk])
        hr = half_rate[:, c0:c0 + LRU_BLOCK]
        log_a = hr + hr * jnp.tanh(0.5 * (ri[:, :LRU_BLOCK] + ba_ref[:, c0:c0 + LRU_BLOCK]))
        hx = 0.5 * xb
        gx = hx + hx * jnp.tanh(0.5 * (ri[:, LRU_BLOCK:] + bx_ref[:, c0:c0 + LRU_BLOCK]))
        a = jnp.exp(log_a)
        th = jnp.tanh(log_a)
        num = -2.0 * th
        root = jnp.where(num > 0.0, num * lax.rsqrt(num), 0.0) * lax.rsqrt(1.0 - th)
        u = root * gx
        for half in range(grp_per_blk):
            g = blk * grp_per_blk + half
            ls = slice(half * LANES, (half + 1) * LANES)
            for k in range(tm // SUBLANES):
                s = k // vregs_per_sub
                jj = (k % vregs_per_sub) * SUBLANES
                dst = pl.ds(jj * SUBLANES + s, SUBLANES, stride=SUBLANES)
                a_scr[g, dst, :] = a[k * SUBLANES:(k + 1) * SUBLANES, ls]
                u_scr[g, dst, :] = u[k * SUBLANES:(k + 1) * SUBLANES, ls]

    def rows_t(j):
        return slice(j * SUBLANES, (j + 1) * SUBLANES)

    for hf in range(LRU_WIDTH // half_w):
        gs = slice(hf * LRU_HALF_GROUPS, (hf + 1) * LRU_HALF_GROUPS)
        h = u_scr[gs, rows_t(steps[0]), :]
        p = a_scr[gs, rows_t(steps[0]), :]
        for j in steps[1:]:
            aj = a_scr[gs, rows_t(j), :]
            h = aj * h + u_scr[gs, rows_t(j), :]
            p = aj * p
            u_scr[gs, rows_t(j), :] = h
            a_scr[gs, rows_t(j), :] = p

        c = carry[gs]
        carry_in = [None] * SUBLANES
        for s in order:
            carry_in[s] = c
            c = p[:, s:s + 1, :] * c + h[:, s:s + 1, :]
        carry[gs] = c
        cmat = jnp.concatenate(carry_in, axis=1)

        for j in range(sub):
            hj = u_scr[gs, rows_t(j), :] + a_scr[gs, rows_t(j), :] * cmat
            for g in range(LRU_HALF_GROUPS):
                cs_ = slice(hf * half_w + g * LANES, hf * half_w + (g + 1) * LANES)
                if combine:
                    n_scr[g, pl.ds(j, SUBLANES, stride=pitch), :] = hj[g] + hb_ref[rows_t(j), cs_]
                else:
                    o_ref[rows_t(j), cs_] = hj[g]

        if combine:
            for s in range(SUBLANES):
                rs = slice(s * sub, (s + 1) * sub)
                for g in range(LRU_HALF_GROUPS):
                    cs_ = slice(hf * half_w + g * LANES, hf * half_w + (g + 1) * LANES)
                    y_scr[rs, cs_] = (n_scr[g, s * pitch:s * pitch + sub, :] * sg_ref[rs, cs_]).astype(BF16)

    if combine:
        out = x_ref[...] + _dot(y_scr[...], wo_ref[...])
        if final:
            out = _rms(out, fw_ref[...])
        o_ref[...] = out


def _lru(xc, w_gates, b_a, b_x, lam, reverse, combine_args=None, final_w=None):
    n_rows = xc.shape[0]
    tm = LRU_TILE
    n = n_rows // tm
    combine = combine_args is not None
    final = final_w is not None
    rows = functools.partial(_row_spec, tm, rev_n=n if reverse else None)
    in_specs = [rows(LRU_WIDTH), _const_spec(w_gates.shape), _const_spec((1, LRU_WIDTH)),
                _const_spec((1, LRU_WIDTH)), _const_spec((1, LRU_WIDTH))]
    args = [xc, w_gates, b_a, b_x, lam]
    n_grp = LRU_WIDTH // LANES
    scratch = [pltpu.VMEM((n_grp, tm, LANES), F32),
               pltpu.VMEM((n_grp, tm, LANES), F32),
               pltpu.VMEM((n_grp, 1, LANES), F32)]
    if combine:
        h_bwd, sg, x, w_out = combine_args
        in_specs += [rows(LRU_WIDTH), rows(LRU_WIDTH), rows(D_MODEL), _const_spec(w_out.shape)]
        args += [h_bwd, sg, x, w_out]
        if final:
            in_specs.append(_const_spec((1, D_MODEL)))
            args.append(final_w)
        scratch.append(pltpu.VMEM((LRU_HALF_GROUPS, SUBLANES * LRU_PITCH, LANES), F32))
        scratch.append(pltpu.VMEM((tm, LRU_WIDTH), BF16))
        out_w = D_MODEL
    else:
        out_w = LRU_WIDTH
    return pl.pallas_call(
        functools.partial(_lru_body, reverse=reverse, combine=combine, final=final),
        grid=(n,),
        in_specs=in_specs, out_specs=rows(out_w),
        out_shape=jax.ShapeDtypeStruct((n_rows, out_w), F32),
        scratch_shapes=scratch,
        compiler_params=_params(), name="lru_fwd" if combine else "lru_bwd",
    )(*args)


def _rope_tables(n_rows):
    inv = ROPE_THETA ** (-jnp.arange(0, ROPE_DIM, 2, dtype=F32) / ROPE_DIM)
    ang = jnp.arange(n_rows, dtype=F32)[:, None] * inv[None, :]
    cos, sin = jnp.cos(ang), jnp.sin(ang)
    half = ROPE_DIM // 2
    ones = jnp.ones((n_rows, ATTN_HEAD_DIM - ROPE_DIM), F32)
    zeros = jnp.zeros((n_rows, ATTN_HEAD_DIM - ROPE_DIM), F32)
    zero_h = jnp.zeros((n_rows, half), F32)
    reps = LANES // ATTN_HEAD_DIM
    cos_t = jnp.tile(jnp.concatenate([cos, cos, ones], axis=1), (1, reps))
    sin_a = jnp.tile(jnp.concatenate([-sin, zero_h, zeros], axis=1), (1, reps))
    sin_b = jnp.tile(jnp.concatenate([zero_h, sin, zeros], axis=1), (1, reps))
    return cos_t, sin_a, sin_b


def _pad_lanes(v):
    return jnp.pad(v.astype(F32), (0, LANES - v.shape[0]))[None, :]


def _even_layer(x, g_norm, w_in, conv_w, conv_b, dt_bias, a_log, d_skip, ssd_norm_w, sink, w_out, rope):
    sizes = (SSD_INNER, SSD_CONV_CH, 2 * SSD_HEADS, D_MODEL, 256, 256, D_MODEL)
    cuts = [0]
    for s in sizes:
        cuts.append(cuts[-1] + s)
    wz, wx, wdt, wq, wk, wv, wg = (w_in[:, cuts[k]:cuts[k + 1]].astype(BF16) for k in range(7))
    wdt = jnp.pad(wdt, ((0, 0), (0, LANES - 2 * SSD_HEADS)))
    dtb = _pad_lanes(dt_bias.reshape(-1))
    z, xc, bm, cm, dt, q, kz, vz, sg = _even_in(
        x, g_norm[None, :], wz, wx, wdt, wq, wk, wv, wg, conv_w, conv_b[None, :], dtb, *rope)

    lane = jnp.arange(LANES)[:, None]
    head = (jnp.arange(SSD_INNER) // SSD_HEAD_DIM)[None, :]
    alog_row = _pad_lanes(a_log.reshape(-1))
    y_bwd = _ssd(xc, bm, cm, dt, alog_row, (lane == head + SSD_HEADS).astype(BF16), reverse=True)
    dskip_x = jnp.repeat(d_skip.astype(F32), SSD_HEAD_DIM)[None, :]
    y_ssd = _ssd(xc, bm, cm, dt, alog_row, (lane == head).astype(BF16), reverse=False,
                 combine_args=(z, y_bwd, dskip_x, ssd_norm_w[None, :]))

    w_out_b = w_out.astype(BF16)
    return _attn_out(sink.astype(F32), q, kz, vz, sg, y_ssd, x, w_out_b[:SSD_INNER], w_out_b[SSD_INNER:])


def _odd_layer(x, g_norm, w_in, conv_w, conv_b, w_a, b_a, w_x, b_x, lam, w_out, final_w):
    w_in_b = w_in.astype(BF16)
    xc, sg = _odd_in(x, g_norm[None, :], w_in_b[:, :LRU_WIDTH], w_in_b[:, LRU_WIDTH:],
                     conv_w, conv_b[None, :])
    w_gates = jnp.concatenate([w_a, w_x], axis=-1).astype(BF16)
    h_bwd = _lru(xc, w_gates[1], b_a[1][None, :], b_x[1][None, :], lam[1][None, :], reverse=True)
    return _lru(xc, w_gates[0], b_a[0][None, :], b_x[0][None, :], lam[0][None, :], reverse=False,
                combine_args=(h_bwd, sg, x, w_out.astype(BF16)),
                final_w=None if final_w is None else final_w[None, :])


def kernel(x, norm_w, final_norm_w, ev_w_in, ev_conv_w, ev_conv_b, ev_dt_bias, ev_a_log, ev_d_skip, ev_ssd_norm_w, ev_sink, ev_w_out, od_w_in, od_conv_w, od_conv_b, od_w_a, od_b_a, od_w_x, od_b_x, od_lambda, od_w_out):
    batch, n_rows, _ = x.shape
    depth = norm_w.shape[0]
    assert batch == 1 and depth % 2 == 0
    rope = _rope_tables(n_rows)
    h = x[0]
    for layer in range(depth):
        j = layer // 2
        if layer % 2 == 0:
            h = _even_layer(h, norm_w[layer], ev_w_in[j], ev_conv_w[j], ev_conv_b[j], ev_dt_bias[j],
                            ev_a_log[j], ev_d_skip[j], ev_ssd_norm_w[j], ev_sink[j], ev_w_out[j], rope)
        else:
            h = _odd_layer(h, norm_w[layer], od_w_in[j], od_conv_w[j], od_conv_b[j], od_w_a[j],
                           od_b_a[j], od_w_x[j], od_b_x[j], od_lambda[j], od_w_out[j],
                           final_norm_w if layer == depth - 1 else None)
    return h[None]
```

```python
import functools

import jax
import jax.numpy as jnp
from jax import lax
from jax.experimental import pallas as pl
from jax.experimental.pallas import tpu as pltpu

F32 = jnp.float32
BF16 = jnp.bfloat16

D_MODEL = 1024
EPS = 1e-6
LANES = 128
SUBLANES = 8
HALO = 16
CONV_W = 4
CONV_LEFT = 2

SSD_HEADS = 16
SSD_HEAD_DIM = 64
SSD_STATE = 128
SSD_GROUPS = 2
SSD_CHUNK = 256
SSD_CHUNKS_PER_STEP = 2
SSD_INNER = 1024
SSD_CONV_CH = 1536

ATTN_HEADS = 16
ATTN_KV_HEADS = 4
ATTN_HEAD_DIM = 64
ATTN_BLOCK = 128
ROPE_THETA = 500000.0
ROPE_DIM = 16

LRU_WIDTH = 2048
LRU_BLOCK = 256
LRU_BLOCKS = 8
LRU_C = 8.0

LOG2E = 1.4426950408889634
NEG = -1e30
VMEM_LIMIT = 56 * 1024 * 1024

IN_TILE_EVEN = 512
IN_TILE_ODD = 512
ATTN_TILE = 512
LRU_TILE = 512
LRU_SUB = LRU_TILE // SUBLANES
LRU_PITCH = LRU_SUB + SUBLANES
LRU_HALF_GROUPS = 8


def _dot(a, b):
    return jnp.dot(a, b, preferred_element_type=F32)


def _split3(v):
    hi = v.astype(BF16)
    r1 = v - hi.astype(F32)
    mid = r1.astype(BF16)
    lo = (r1 - mid.astype(F32)).astype(BF16)
    return hi, mid, lo


def _dot3(*ops, rhs_split=True):
    if rhs_split:
        m, parts = ops[0], ops[1:]
        return _dot(m, parts[0]) + _dot(m, parts[1]) + _dot(m, parts[2])
    parts, m = ops[:3], ops[3]
    return _dot(parts[0], m) + _dot(parts[1], m) + _dot(parts[2], m)


def _rms(x, g):
    ms = jnp.mean(x * x, axis=-1, keepdims=True)
    return x * lax.rsqrt(ms + EPS) * g


def _sigmoid(x):
    return 0.5 * jnp.tanh(0.5 * x) + 0.5


def _silu(x):
    h = 0.5 * x
    return h + h * jnp.tanh(h)


def _params(n_axes=1):
    return pltpu.CompilerParams(dimension_semantics=("arbitrary",) * n_axes,
                                vmem_limit_bytes=VMEM_LIMIT)


def _const_spec(shape):
    nd = len(shape)
    return pl.BlockSpec(shape, lambda i: (0,) * nd, pipeline_mode=pl.Buffered(1))


def _row_spec(tm, width, rev_n=None):
    if rev_n is None:
        return pl.BlockSpec((tm, width), lambda i: (i, 0))
    return pl.BlockSpec((tm, width), lambda i: (rev_n - 1 - i, 0))


def _halo_specs(tm, n_rows):
    hb = tm // HALO
    nh = n_rows // HALO
    prev = pl.BlockSpec((HALO, D_MODEL), lambda i: (jnp.maximum(i * hb - 1, 0), 0))
    main = pl.BlockSpec((tm, D_MODEL), lambda i: (i, 0))
    nxt = pl.BlockSpec((HALO, D_MODEL), lambda i: (jnp.minimum((i + 1) * hb, nh - 1), 0))
    return [prev, main, nxt]


def _fill_normed(xp_ref, xm_ref, xn_ref, g_ref, h_scr, tm):
    i = pl.program_id(0)
    n = pl.num_programs(0)
    g = g_ref[...]
    hp = _rms(xp_ref[...], g)
    hn = _rms(xn_ref[...], g)
    h_scr[0:HALO, :] = jnp.where(i > 0, hp, 0.0).astype(BF16)
    h_scr[HALO:HALO + tm, :] = _rms(xm_ref[...], g).astype(BF16)
    h_scr[HALO + tm:HALO + tm + HALO, :] = jnp.where(i < n - 1, hn, 0.0).astype(BF16)


def _proj_conv(h_ext, w_ref, u_scr, cw_ref, cb_ref, tm, c0, width):
    u = _dot(h_ext, w_ref[:, c0:c0 + width])
    base = HALO - CONV_LEFT
    outs = []
    for j in range(width // LANES):
        slab = c0 // LANES + j
        u_scr[slab] = u[:, j * LANES:(j + 1) * LANES]
        cs_ = slice(c0 + j * LANES, c0 + (j + 1) * LANES)
        acc = cb_ref[:, cs_] + cw_ref[0:1, cs_] * u_scr[slab, base:base + tm, :]
        for k in range(1, CONV_W):
            acc = acc + cw_ref[k:k + 1, cs_] * u_scr[slab, base + k:base + k + tm, :]
        outs.append(acc)
    return outs


def _even_in_body(xp_ref, xm_ref, xn_ref, g_ref, wz_ref, wx_ref, wdt_ref, wq_ref, wk_ref, wv_ref,
                  wg_ref, cw_ref, cb_ref, dtb_ref, cos_ref, sa_ref, sb_ref,
                  z_ref, xc_ref, b_ref, c_ref, dt_ref, q_ref, kz_ref, vz_ref, sg_ref,
                  h_scr, u_scr, *, tm):
    _fill_normed(xp_ref, xm_ref, xn_ref, g_ref, h_scr, tm)
    h_ext = h_scr[...]
    hm = h_scr[HALO:HALO + tm, :]
    cstep = 256
    cos_t = cos_ref[...]
    sin_a = sa_ref[...]
    sin_b = sb_ref[...]
    low = lax.broadcasted_iota(jnp.int32, (tm, LANES), 1) < ATTN_HEAD_DIM

    def rope(t):
        return (t * cos_t + pltpu.roll(t, LANES - ROPE_DIM // 2, 1) * sin_a
                + pltpu.roll(t, ROPE_DIM // 2, 1) * sin_b)

    def do_xbc(c0):
        outs = _proj_conv(h_ext, wx_ref, u_scr, cw_ref, cb_ref, tm, c0, cstep)
        for j, acc in enumerate(outs):
            y = _silu(acc)
            c = c0 + j * LANES
            if c < SSD_INNER:
                xc_ref[:, c:c + LANES] = y
            elif c < SSD_INNER + 256:
                b_ref[:, c - SSD_INNER:c - SSD_INNER + LANES] = y
            else:
                c_ref[:, c - SSD_INNER - 256:c - SSD_INNER - 256 + LANES] = y

    def do_z(c0):
        z_ref[:, c0:c0 + cstep] = _dot(hm, wz_ref[:, c0:c0 + cstep])

    def do_gate(c0):
        sg_ref[:, c0:c0 + cstep] = _silu(_dot(hm, wg_ref[:, c0:c0 + cstep]))

    def do_q(c0):
        qf = _dot(hm, wq_ref[:, c0:c0 + cstep]) * (ATTN_HEAD_DIM ** -0.5)
        for j in range(cstep // LANES):
            q_ref[:, c0 + j * LANES:c0 + (j + 1) * LANES] = rope(qf[:, j * LANES:(j + 1) * LANES]).astype(BF16)

    def do_kv(w_ref, dst, rotary):
        f = _dot(hm, w_ref[...])
        for j in range(2):
            a = f[:, j * LANES:(j + 1) * LANES]
            if rotary:
                a = rope(a)
            r = pltpu.roll(a, ATTN_HEAD_DIM, 1)
            c = 4 * j * LANES
            dst[:, c:c + LANES] = jnp.where(low, a, 0.0).astype(BF16)
            dst[:, c + LANES:c + 2 * LANES] = jnp.where(low, 0.0, r).astype(BF16)
            dst[:, c + 2 * LANES:c + 3 * LANES] = jnp.where(low, r, 0.0).astype(BF16)
            dst[:, c + 3 * LANES:c + 4 * LANES] = jnp.where(low, 0.0, a).astype(BF16)

    for n in range(4):
        do_xbc(n * cstep)
        do_z(n * cstep)
        do_q(n * cstep)
        do_gate(n * cstep)
        if n == 0:
            do_xbc(4 * cstep)
        elif n == 1:
            do_xbc(5 * cstep)
        elif n == 2:
            do_kv(wk_ref, kz_ref, True)
        else:
            do_kv(wv_ref, vz_ref, False)
    dt_ref[...] = jax.nn.softplus(_dot(hm, wdt_ref[...]) + dtb_ref[...])


def _even_in(x, g, wz, wx, wdt, wq, wk, wv, wg, cw, cb, dtb, cos_t, sin_a, sin_b):
    n_rows = x.shape[0]
    tm = IN_TILE_EVEN
    rows = functools.partial(_row_spec, tm)
    in_specs = _halo_specs(tm, n_rows) + [
        _const_spec((1, D_MODEL)),
        _const_spec(wz.shape), _const_spec(wx.shape), _const_spec(wdt.shape), _const_spec(wq.shape),
        _const_spec(wk.shape), _const_spec(wv.shape), _const_spec(wg.shape),
        _const_spec(cw.shape), _const_spec(cb.shape), _const_spec(dtb.shape),
        rows(LANES), rows(LANES), rows(LANES),
    ]
    out_shape = (
        jax.ShapeDtypeStruct((n_rows, SSD_INNER), F32),
        jax.ShapeDtypeStruct((n_rows, SSD_INNER), F32),
        jax.ShapeDtypeStruct((n_rows, 256), F32),
        jax.ShapeDtypeStruct((n_rows, 256), F32),
        jax.ShapeDtypeStruct((n_rows, LANES), F32),
        jax.ShapeDtypeStruct((n_rows, D_MODEL), BF16),
        jax.ShapeDtypeStruct((n_rows, D_MODEL), BF16),
        jax.ShapeDtypeStruct((n_rows, D_MODEL), BF16),
        jax.ShapeDtypeStruct((n_rows, D_MODEL), F32),
    )
    out_specs = (rows(SSD_INNER), rows(SSD_INNER), rows(256), rows(256), rows(LANES),
                 rows(D_MODEL), rows(D_MODEL), rows(D_MODEL), rows(D_MODEL))
    return pl.pallas_call(
        functools.partial(_even_in_body, tm=tm),
        grid=(n_rows // tm,),
        in_specs=in_specs, out_specs=out_specs, out_shape=out_shape,
        scratch_shapes=[pltpu.VMEM((tm + 2 * HALO, D_MODEL), BF16),
                        pltpu.VMEM((SSD_CONV_CH // LANES, tm + 2 * HALO, LANES), F32)],
        compiler_params=_params(), name="even_in",
    )(x, x, x, g, wz, wx, wdt, wq, wk, wv, wg, cw, cb, dtb, cos_t, sin_a, sin_b)


def _ssd_body(*refs, reverse, combine):
    if combine:
        (xc_ref, b_ref, c_ref, dt_ref, alog_ref, e_ref, z_ref, yb_ref, dsk_ref, nw_ref,
         y_ref, state, cb_scr, yoff_scr, ytmp) = refs
    else:
        xc_ref, b_ref, c_ref, dt_ref, alog_ref, e_ref, y_ref, state, cb_scr, yoff_scr = refs
        z_ref = yb_ref = dsk_ref = nw_ref = ytmp = None

    @pl.when(pl.program_id(0) == 0)
    def _():
        state[...] = jnp.zeros_like(state)

    chunks = list(range(SSD_CHUNKS_PER_STEP))
    if reverse:
        chunks = chunks[::-1]
    for ci in chunks:
        rs = pl.ds(ci * SSD_CHUNK, SSD_CHUNK)
        y_view = y_ref.at[rs]
        _ssd_chunk(xc_ref.at[rs], b_ref.at[rs], c_ref.at[rs], dt_ref.at[rs], alog_ref, e_ref,
                   z_ref.at[rs] if combine else None, yb_ref.at[rs] if combine else None,
                   dsk_ref, nw_ref, y_view, state, cb_scr, yoff_scr,
                   ytmp if combine else y_view, reverse=reverse, combine=combine)


def _ssd_chunk(xc_ref, b_ref, c_ref, dt_ref, alog_ref, e_ref, z_ref, yb_ref, dsk_ref, nw_ref,
               y_ref, state, cb_scr, yoff_scr, ytmp, *, reverse, combine):
    t_len = SSD_CHUNK
    off = SSD_HEADS if reverse else 0

    row = lax.broadcasted_iota(jnp.int32, (t_len, t_len), 0)
    col = lax.broadcasted_iota(jnp.int32, (t_len, t_len), 1)
    mask = (col >= row) if reverse else (col <= row)
    tri = jnp.where(mask, 1.0, 0.0).astype(BF16)

    lane = lax.broadcasted_iota(jnp.int32, (1, LANES), 1)
    in_dir = (lane >= off) & (lane < off + SSD_HEADS)
    a_rate = jnp.where(in_dir, -LOG2E * jnp.exp(alog_ref[...]), 0.0)

    dt = dt_ref[...]
    cs = _dot3(tri, *_split3(dt * a_rate))
    log_dt = jnp.log2(dt)
    cs_t = (cs - log_dt).T
    last = 0 if reverse else t_len - 1
    cs_last = cs[last:last + 1, :]
    w_hi, w_mid, _ = _split3(jnp.exp2(cs_last - cs + log_dt))
    expand = e_ref[...]
    wdt_x = _dot(w_hi, expand) + _dot(w_mid, expand)

    b_f = b_ref[...]
    c_b = c_ref[...].astype(BF16)
    b_b = b_f.astype(BF16)
    b_t = b_f.T.astype(BF16)
    low = lax.broadcasted_iota(jnp.int32, (t_len, LANES), 1) < SSD_HEAD_DIM

    hb = t_len // 2
    key_blocks = ((0, 1), (1,)) if reverse else ((0,), (0, 1))
    diag_mask = mask[0:hb, 0:hb]
    low_hb = lax.broadcasted_iota(jnp.int32, (hb, LANES), 1) < SSD_HEAD_DIM

    gw = SSD_INNER // SSD_GROUPS
    for g in range(SSD_GROUPS):
        c_g = c_b[:, g * SSD_STATE:(g + 1) * SSD_STATE]
        b_g = b_b[:, g * SSD_STATE:(g + 1) * SSD_STATE]
        cb_scr[...] = lax.dot_general(c_g, b_g, (((1,), (1,)), ((), ())), preferred_element_type=F32)
        s_g = state[:, g * gw:(g + 1) * gw]
        yoff_scr[...] = _dot(c_g, s_g.astype(BF16))
        edge_decay = []
        for tt in range(gw // LANES):
            t = g * (gw // LANES) + tt
            ts = slice(t * LANES, (t + 1) * LANES)
            xt = xc_ref[:, ts]
            x_halves = (jnp.where(low, xt, 0.0).astype(BF16), jnp.where(low, 0.0, xt).astype(BF16))
            for rb in range(2):
                rs = slice(rb * hb, (rb + 1) * hb)
                cols = [jnp.broadcast_to(cs[rs, off + 2 * t + half:off + 2 * t + half + 1], (hb, LANES))
                        for half in range(2)]
                e_tile = jnp.where(low_hb, jnp.exp2(cols[0]), jnp.exp2(cols[1]))
                if rb * hb <= last < (rb + 1) * hb:
                    edge_decay.append(e_tile[last - rb * hb:last - rb * hb + 1, :])
                acc = yoff_scr[rs, tt * LANES:(tt + 1) * LANES] * e_tile
                for half in range(2):
                    ln = off + 2 * t + half
                    pieces = []
                    for kb in key_blocks[rb]:
                        ks = slice(kb * hb, (kb + 1) * hb)
                        seg = cols[half] - cs_t[ln:ln + 1, ks]
                        if kb == rb:
                            seg = jnp.where(diag_mask, seg, NEG)
                        pieces.append((cb_scr[rs, ks] * jnp.exp2(seg)).astype(BF16))
                    m = pieces[0] if len(pieces) == 1 else jnp.concatenate(pieces, axis=1)
                    k0, k1 = key_blocks[rb][0] * hb, (key_blocks[rb][-1] + 1) * hb
                    acc = acc + _dot(m, x_halves[half][k0:k1, :])
                ytmp[rs, ts] = acc
        gs = slice(g * gw, (g + 1) * gw)
        xw_b = (xc_ref[:, gs] * wdt_x[:, gs]).astype(BF16)
        state[:, gs] = (jnp.concatenate(edge_decay, axis=1) * s_g
                        + _dot(b_t[g * SSD_STATE:(g + 1) * SSD_STATE, :], xw_b))

    if combine:
        y = ytmp[...] + yb_ref[...] + xc_ref[...] * dsk_ref[...]
        y = y * _silu(z_ref[...])
        for g in range(SSD_GROUPS):
            yg = y[:, g * gw:(g + 1) * gw]
            ms = jnp.mean(yg * yg, axis=-1, keepdims=True)
            y_ref[:, g * gw:(g + 1) * gw] = (
                yg * lax.rsqrt(ms + EPS) * nw_ref[:, g * gw:(g + 1) * gw]).astype(BF16)


def _ssd(xc, bm, cm, dt, alog_row, expand, reverse, combine_args=None):
    n_rows = xc.shape[0]
    t_len = SSD_CHUNK
    tile = SSD_CHUNKS_PER_STEP * t_len
    nc = n_rows // tile
    rows = functools.partial(_row_spec, tile, rev_n=nc if reverse else None)
    combine = combine_args is not None
    in_specs = [rows(SSD_INNER), rows(256), rows(256), rows(LANES),
                _const_spec((1, LANES)), _const_spec((LANES, SSD_INNER))]
    args = [xc, bm, cm, dt, alog_row, expand]
    scratch = [pltpu.VMEM((SSD_STATE, SSD_INNER), F32),
               pltpu.VMEM((t_len, t_len), F32),
               pltpu.VMEM((t_len, SSD_INNER // SSD_GROUPS), F32)]
    if combine:
        z, y_bwd, dskip_x, norm_w = combine_args
        in_specs += [rows(SSD_INNER), rows(SSD_INNER),
                     _const_spec((1, SSD_INNER)), _const_spec((1, SSD_INNER))]
        args += [z, y_bwd, dskip_x, norm_w]
        scratch.append(pltpu.VMEM((t_len, SSD_INNER), F32))
        out_dtype = BF16
    else:
        out_dtype = F32
    return pl.pallas_call(
        functools.partial(_ssd_body, reverse=reverse, combine=combine),
        grid=(nc,),
        in_specs=in_specs, out_specs=rows(SSD_INNER),
        out_shape=jax.ShapeDtypeStruct((n_rows, SSD_INNER), out_dtype),
        scratch_shapes=scratch,
        compiler_params=_params(), name="ssd_fwd" if combine else "ssd_bwd",
    )(*args)


def _attn_body(sink_ref, q_ref, kp_ref, km_ref, kn_ref, vp_ref, vm_ref, vn_ref, sg_ref, ys_ref,
               x_ref, w1_ref, w2_ref, o_ref, k_scr, v_scr, y_scr, yprev_scr, s_scr, *, tq):
    i = pl.program_id(0)
    n = pl.num_programs(0) - 1
    tile = jnp.minimum(i, n - 1)
    blk = ATTN_BLOCK
    nq = tq // blk
    wc = D_MODEL // nq

    @pl.when(i == 0)
    def _():
        y_scr[...] = jnp.zeros_like(y_scr)

    yprev_scr[...] = y_scr[...]
    for scr, p_ref, m_ref, n_ref in ((k_scr, kp_ref, km_ref, kn_ref), (v_scr, vp_ref, vm_ref, vn_ref)):
        scr[0:blk, :] = p_ref[...]
        scr[blk:blk + tq, :] = m_ref[...]
        scr[blk + tq:blk + tq + blk, :] = n_ref[...]

    row2 = lax.broadcasted_iota(jnp.int32, (2 * blk, blk), 0)
    qi = row2 & (blk - 1)
    kj = lax.broadcasted_iota(jnp.int32, (2 * blk, blk), 1)
    first_tile = row2[:, 0:1] < blk
    n_kv = ATTN_KV_HEADS

    for b in range(nq):
        r0 = b * blk
        gb = tile * nq + b
        bias_l = jnp.where((kj >= qi) & (gb > 0), 0.0, NEG)
        bias_r = jnp.where((kj <= qi) & (gb < n * nq - 1), 0.0, NEG)
        for hk in range(n_kv):
            q2 = jnp.concatenate(
                [q_ref[pl.ds(r0, blk), (2 * hk) * LANES:(2 * hk + 1) * LANES],
                 q_ref[pl.ds(r0, blk), (2 * hk + 1) * LANES:(2 * hk + 2) * LANES]], axis=0)
            kz = jnp.concatenate(
                [k_scr[pl.ds(r0, 3 * blk), (2 * hk + half) * LANES:(2 * hk + half + 1) * LANES]
                 for half in range(2)], axis=0)
            s_scr[hk] = lax.dot_general(q2, kz, (((1,), (1,)), ((), ())), preferred_element_type=F32)
        oc = slice(b * wc, (b + 1) * wc)
        o_ref[:, oc] = (x_ref[:, oc] + _dot(ys_ref[...], w1_ref[:, oc])
                        + _dot(yprev_scr[...], w2_ref[:, oc]))
        for hk in range(n_kv):
            out = None
            for half in range(2):
                c = (2 * hk + half) * LANES
                k0 = half * 3 * blk
                s_l = s_scr[hk, :, k0:k0 + blk] + bias_l
                s_m = s_scr[hk, :, k0 + blk:k0 + 2 * blk]
                s_r = s_scr[hk, :, k0 + 2 * blk:k0 + 3 * blk] + bias_r
                sink = jnp.where(first_tile, sink_ref[4 * hk + half], sink_ref[4 * hk + 2 + half])
                m = jnp.max(jnp.maximum(jnp.maximum(s_l, s_m), s_r), axis=-1, keepdims=True)
                m = jnp.maximum(m, sink)
                p_l = jnp.exp(s_l - m)
                p_m = jnp.exp(s_m - m)
                p_r = jnp.exp(s_r - m)
                den = jnp.sum(p_l + p_m + p_r, axis=-1, keepdims=True) + jnp.exp(sink - m)
                p = jnp.concatenate([p_l, p_m, p_r], axis=1).astype(BF16)
                o = _dot(p, v_scr[pl.ds(r0, 3 * blk), c:c + LANES]) * (1.0 / den)
                out = o if out is None else out + o
            for k in range(2):
                t = 2 * hk + k
                y_scr[pl.ds(r0, blk), t * LANES:(t + 1) * LANES] = (
                    out[k * blk:(k + 1) * blk, :]
                    * sg_ref[pl.ds(r0, blk), t * LANES:(t + 1) * LANES]).astype(BF16)


def _attn_out(sink, q, kz, vz, sg, y_ssd, x, w1, w2):
    n_rows = x.shape[0]
    tq = ATTN_TILE
    blk = ATTN_BLOCK
    per = tq // blk
    nb = n_rows // blk
    n = n_rows // tq

    def cur(i):
        return jnp.minimum(i, n - 1)

    def lag(i):
        return jnp.maximum(i - 1, 0)

    rows_cur = pl.BlockSpec((tq, D_MODEL), lambda i: (cur(i), 0))
    rows_lag = pl.BlockSpec((tq, D_MODEL), lambda i: (lag(i), 0))
    prev = pl.BlockSpec((blk, D_MODEL), lambda i: (jnp.maximum(cur(i) * per - 1, 0), 0))
    nxt = pl.BlockSpec((blk, D_MODEL), lambda i: (jnp.minimum((cur(i) + 1) * per, nb - 1), 0))
    in_specs = [pl.BlockSpec(memory_space=pltpu.SMEM),
                rows_cur, prev, rows_cur, nxt, prev, rows_cur, nxt,
                rows_cur, rows_lag, rows_lag,
                _const_spec(w1.shape), _const_spec(w2.shape)]
    return pl.pallas_call(
        functools.partial(_attn_body, tq=tq),
        grid=(n + 1,),
        in_specs=in_specs, out_specs=rows_lag,
        out_shape=jax.ShapeDtypeStruct((n_rows, D_MODEL), F32),
        scratch_shapes=[pltpu.VMEM((tq + 2 * blk, D_MODEL), BF16),
                        pltpu.VMEM((tq + 2 * blk, D_MODEL), BF16),
                        pltpu.VMEM((tq, D_MODEL), BF16),
                        pltpu.VMEM((tq, D_MODEL), BF16),
                        pltpu.VMEM((ATTN_KV_HEADS, 2 * blk, 6 * blk), F32)],
        compiler_params=_params(), name="attn_out",
    )(sink, q, kz, kz, kz, vz, vz, vz, sg, y_ssd, x, w1, w2)


def _odd_in_body(xp_ref, xm_ref, xn_ref, g_ref, wx_ref, wg_ref, cw_ref, cb_ref,
                 xc_ref, sg_ref, h_scr, u_scr, *, tm):
    _fill_normed(xp_ref, xm_ref, xn_ref, g_ref, h_scr, tm)
    h_ext = h_scr[...]
    hm = h_scr[HALO:HALO + tm, :]
    cstep = 256
    for c0 in range(0, LRU_WIDTH, cstep):
        outs = _proj_conv(h_ext, wx_ref, u_scr, cw_ref, cb_ref, tm, c0, cstep)
        for j, acc in enumerate(outs):
            xc_ref[:, c0 + j * LANES:c0 + (j + 1) * LANES] = acc
        sg_ref[:, c0:c0 + cstep] = _silu(_dot(hm, wg_ref[:, c0:c0 + cstep]))


def _odd_in(x, g, wx, wg, cw, cb):
    n_rows = x.shape[0]
    tm = IN_TILE_ODD
    rows = functools.partial(_row_spec, tm)
    in_specs = _halo_specs(tm, n_rows) + [
        _const_spec((1, D_MODEL)), _const_spec(wx.shape), _const_spec(wg.shape),
        _const_spec(cw.shape), _const_spec(cb.shape)]
    return pl.pallas_call(
        functools.partial(_odd_in_body, tm=tm),
        grid=(n_rows // tm,),
        in_specs=in_specs, out_specs=(rows(LRU_WIDTH), rows(LRU_WIDTH)),
        out_shape=(jax.ShapeDtypeStruct((n_rows, LRU_WIDTH), F32),
                   jax.ShapeDtypeStruct((n_rows, LRU_WIDTH), F32)),
        scratch_shapes=[pltpu.VMEM((tm + 2 * HALO, D_MODEL), BF16),
                        pltpu.VMEM((LRU_WIDTH // LANES, tm + 2 * HALO, LANES), F32)],
        compiler_params=_params(), name="odd_in",
    )(x, x, x, g, wx, wg, cw, cb)


def _lru_body(*refs, reverse, combine, final):
    if combine:
        if final:
            (xc_ref, w_ref, ba_ref, bx_ref, lam_ref, hb_ref, sg_ref, x_ref, wo_ref, fw_ref,
             o_ref, a_scr, u_scr, carry, n_scr, y_scr) = refs
        else:
            (xc_ref, w_ref, ba_ref, bx_ref, lam_ref, hb_ref, sg_ref, x_ref, wo_ref,
             o_ref, a_scr, u_scr, carry, n_scr, y_scr) = refs
    else:
        xc_ref, w_ref, ba_ref, bx_ref, lam_ref, o_ref, a_scr, u_scr, carry = refs
    tm, sub, pitch = LRU_TILE, LRU_SUB, LRU_PITCH
    vregs_per_sub = sub // SUBLANES
    grp_per_blk = LRU_BLOCK // LANES
    half_w = LRU_HALF_GROUPS * LANES

    @pl.when(pl.program_id(0) == 0)
    def _():
        carry[...] = jnp.zeros_like(carry)

    half_rate = (-0.5 * LRU_C) * jax.nn.softplus(-lam_ref[...])
    steps = list(range(sub))
    order = list(range(SUBLANES))
    if reverse:
        steps = steps[::-1]
        order = order[::-1]

    for blk in range(LRU_BLOCKS):
        c0 = blk * LRU_BLOCK
        xb = xc_ref[:, c0:c0 + LRU_BLOCK]
        ri = _dot(xb.astype(BF16), w_ref[blk])
        hr = half_rate[:, c0:c0 + LRU_BLOCK]
        log_a = hr + hr * jnp.tanh(0.5 * (ri[:, :LRU_BLOCK] + ba_ref[:, c0:c0 + LRU_BLOCK]))
        hx = 0.5 * xb
        gx = hx + hx * jnp.tanh(0.5 * (ri[:, LRU_BLOCK:] + bx_ref[:, c0:c0 + LRU_BLOCK]))
        a = jnp.exp(log_a)
        th = jnp.tanh(log_a)
        num = -2.0 * th
        root = jnp.where(num > 0.0, num * lax.rsqrt(num), 0.0) * lax.rsqrt(1.0 - th)
        u = root * gx
        for half in range(grp_per_blk):
            g = blk * grp_per_blk + half
            ls = slice(half * LANES, (half + 1) * LANES)
            for k in range(tm // SUBLANES):
                s = k // vregs_per_sub
                jj = (k % vregs_per_sub) * SUBLANES
                dst = pl.ds(jj * SUBLANES + s, SUBLANES, stride=SUBLANES)
                a_scr[g, dst, :] = a[k * SUBLANES:(k + 1) * SUBLANES, ls]
                u_scr[g, dst, :] = u[k * SUBLANES:(k + 1) * SUBLANES, ls]

    def rows_t(j):
        return slice(j * SUBLANES, (j + 1) * SUBLANES)

    for hf in range(LRU_WIDTH // half_w):
        gs = slice(hf * LRU_HALF_GROUPS, (hf + 1) * LRU_HALF_GROUPS)
        h = u_scr[gs, rows_t(steps[0]), :]
        p = a_scr[gs, rows_t(steps[0]), :]
        for j in steps[1:]:
            aj = a_scr[gs, rows_t(j), :]
            h = aj * h + u_scr[gs, rows_t(j), :]
            p = aj * p
            u_scr[gs, rows_t(j), :] = h
            a_scr[gs, rows_t(j), :] = p

        c = carry[gs]
        carry_in = [None] * SUBLANES
        for s in order:
            carry_in[s] = c
            c = p[:, s:s + 1, :] * c + h[:, s:s + 1, :]
        carry[gs] = c
        cmat = jnp.concatenate(carry_in, axis=1)

        for j in range(sub):
            hj = u_scr[gs, rows_t(j), :] + a_scr[gs, rows_t(j), :] * cmat
            for g in range(LRU_HALF_GROUPS):
                cs_ = slice(hf * half_w + g * LANES, hf * half_w + (g + 1) * LANES)
                if combine:
                    n_scr[g, pl.ds(j, SUBLANES, stride=pitch), :] = hj[g] + hb_ref[rows_t(j), cs_]
                else:
                    o_ref[rows_t(j), cs_] = hj[g]

        if combine:
            for s in range(SUBLANES):
                rs = slice(s * sub, (s + 1) * sub)
                for g in range(LRU_HALF_GROUPS):
                    cs_ = slice(hf * half_w + g * LANES, hf * half_w + (g + 1) * LANES)
                    y_scr[rs, cs_] = (n_scr[g, s * pitch:s * pitch + sub, :] * sg_ref[rs, cs_]).astype(BF16)

    if combine:
        out = x_ref[...] + _dot(y_scr[...], wo_ref[...])
        if final:
            out = _rms(out, fw_ref[...])
        o_ref[...] = out


def _lru(xc, w_gates, b_a, b_x, lam, reverse, combine_args=None, final_w=None):
    n_rows = xc.shape[0]
    tm = LRU_TILE
    n = n_rows // tm
    combine = combine_args is not None
    final = final_w is not None
    rows = functools.partial(_row_spec, tm, rev_n=n if reverse else None)
    in_specs = [rows(LRU_WIDTH), _const_spec(w_gates.shape), _const_spec((1, LRU_WIDTH)),
                _const_spec((1, LRU_WIDTH)), _const_spec((1, LRU_WIDTH))]
    args = [xc, w_gates, b_a, b_x, lam]
    n_grp = LRU_WIDTH // LANES
    scratch = [pltpu.VMEM((n_grp, tm, LANES), F32),
               pltpu.VMEM((n_grp, tm, LANES), F32),
               pltpu.VMEM((n_grp, 1, LANES), F32)]
    if combine:
        h_bwd, sg, x, w_out = combine_args
        in_specs += [rows(LRU_WIDTH), rows(LRU_WIDTH), rows(D_MODEL), _const_spec(w_out.shape)]
        args += [h_bwd, sg, x, w_out]
        if final:
            in_specs.append(_const_spec((1, D_MODEL)))
            args.append(final_w)
        scratch.append(pltpu.VMEM((LRU_HALF_GROUPS, SUBLANES * LRU_PITCH, LANES), F32))
        scratch.append(pltpu.VMEM((tm, LRU_WIDTH), BF16))
        out_w = D_MODEL
    else:
        out_w = LRU_WIDTH
    return pl.pallas_call(
        functools.partial(_lru_body, reverse=reverse, combine=combine, final=final),
        grid=(n,),
        in_specs=in_specs, out_specs=rows(out_w),
        out_shape=jax.ShapeDtypeStruct((n_rows, out_w), F32),
        scratch_shapes=scratch,
        compiler_params=_params(), name="lru_fwd" if combine else "lru_bwd",
    )(*args)


def _rope_tables(n_rows):
    inv = ROPE_THETA ** (-jnp.arange(0, ROPE_DIM, 2, dtype=F32) / ROPE_DIM)
    ang = jnp.arange(n_rows, dtype=F32)[:, None] * inv[None, :]
    cos, sin = jnp.cos(ang), jnp.sin(ang)
    half = ROPE_DIM // 2
    ones = jnp.ones((n_rows, ATTN_HEAD_DIM - ROPE_DIM), F32)
    zeros = jnp.zeros((n_rows, ATTN_HEAD_DIM - ROPE_DIM), F32)
    zero_h = jnp.zeros((n_rows, half), F32)
    reps = LANES // ATTN_HEAD_DIM
    cos_t = jnp.tile(jnp.concatenate([cos, cos, ones], axis=1), (1, reps))
    sin_a = jnp.tile(jnp.concatenate([-sin, zero_h, zeros], axis=1), (1, reps))
    sin_b = jnp.tile(jnp.concatenate([zero_h, sin, zeros], axis=1), (1, reps))
    return cos_t, sin_a, sin_b


def _pad_lanes(v):
    return jnp.pad(v.astype(F32), (0, LANES - v.shape[0]))[None, :]


def _even_layer(x, g_norm, w_in, conv_w, conv_b, dt_bias, a_log, d_skip, ssd_norm_w, sink, w_out, rope):
    sizes = (SSD_INNER, SSD_CONV_CH, 2 * SSD_HEADS, D_MODEL, 256, 256, D_MODEL)
    cuts = [0]
    for s in sizes:
        cuts.append(cuts[-1] + s)
    wz, wx, wdt, wq, wk, wv, wg = (w_in[:, cuts[k]:cuts[k + 1]].astype(BF16) for k in range(7))
    wdt = jnp.pad(wdt, ((0, 0), (0, LANES - 2 * SSD_HEADS)))
    dtb = _pad_lanes(dt_bias.reshape(-1))
    z, xc, bm, cm, dt, q, kz, vz, sg = _even_in(
        x, g_norm[None, :], wz, wx, wdt, wq, wk, wv, wg, conv_w, conv_b[None, :], dtb, *rope)

    lane = jnp.arange(LANES)[:, None]
    head = (jnp.arange(SSD_INNER) // SSD_HEAD_DIM)[None, :]
    alog_row = _pad_lanes(a_log.reshape(-1))
    y_bwd = _ssd(xc, bm, cm, dt, alog_row, (lane == head + SSD_HEADS).astype(BF16), reverse=True)
    dskip_x = jnp.repeat(d_skip.astype(F32), SSD_HEAD_DIM)[None, :]
    y_ssd = _ssd(xc, bm, cm, dt, alog_row, (lane == head).astype(BF16), reverse=False,
                 combine_args=(z, y_bwd, dskip_x, ssd_norm_w[None, :]))

    w_out_b = w_out.astype(BF16)
    return _attn_out(sink.astype(F32), q, kz, vz, sg, y_ssd, x, w_out_b[:SSD_INNER], w_out_b[SSD_INNER:])


def _odd_layer(x, g_norm, w_in, conv_w, conv_b, w_a, b_a, w_x, b_x, lam, w_out, final_w):
    w_in_b = w_in.astype(BF16)
    xc, sg = _odd_in(x, g_norm[None, :], w_in_b[:, :LRU_WIDTH], w_in_b[:, LRU_WIDTH:],
                     conv_w, conv_b[None, :])
    w_gates = jnp.concatenate([w_a, w_x], axis=-1).astype(BF16)
    h_bwd = _lru(xc, w_gates[1], b_a[1][None, :], b_x[1][None, :], lam[1][None, :], reverse=True)
    return _lru(xc, w_gates[0], b_a[0][None, :], b_x[0][None, :], lam[0][None, :], reverse=False,
                combine_args=(h_bwd, sg, x, w_out.astype(BF16)),
                final_w=None if final_w is None else final_w[None, :])


def kernel(x, norm_w, final_norm_w, ev_w_in, ev_conv_w, ev_conv_b, ev_dt_bias, ev_a_log, ev_d_skip, ev_ssd_norm_w, ev_sink, ev_w_out, od_w_in, od_conv_w, od_conv_b, od_w_a, od_b_a, od_w_x, od_b_x, od_lambda, od_w_out):
    batch, n_rows, _ = x.shape
    depth = norm_w.shape[0]
    assert batch == 1 and depth % 2 == 0
    rope = _rope_tables(n_rows)
    h = x[0]
    for layer in range(depth):
        j = layer // 2
        if layer % 2 == 0:
            h = _even_layer(h, norm_w[layer], ev_w_in[j], ev_conv_w[j], ev_conv_b[j], ev_dt_bias[j],
                            ev_a_log[j], ev_d_skip[j], ev_ssd_norm_w[j], ev_sink[j], ev_w_out[j], rope)
        else:
            h = _odd_layer(h, norm_w[layer], od_w_in[j], od_conv_w[j], od_conv_b[j], od_w_a[j],
                           od_b_a[j], od_w_x[j], od_b_x[j], od_lambda[j], od_w_out[j],
                           final_norm_w if layer == depth - 1 else None)
    return h[None]
```

```python
import functools

import jax
import jax.numpy as jnp
from jax import lax
from jax.experimental import pallas as pl
from jax.experimental.pallas import tpu as pltpu

F32 = jnp.float32
BF16 = jnp.bfloat16

D_MODEL = 1024
EPS = 1e-6
LANES = 128
SUBLANES = 8
HALO = 16
CONV_W = 4
CONV_LEFT = 2

SSD_HEADS = 16
SSD_HEAD_DIM = 64
SSD_STATE = 128
SSD_GROUPS = 2
SSD_CHUNK = 256
SSD_CHUNKS_PER_STEP = 2
SSD_INNER = 1024
SSD_CONV_CH = 1536

ATTN_HEADS = 16
ATTN_KV_HEADS = 4
ATTN_HEAD_DIM = 64
ATTN_BLOCK = 128
ROPE_THETA = 500000.0
ROPE_DIM = 16

LRU_WIDTH = 2048
LRU_BLOCK = 256
LRU_BLOCKS = 8
LRU_C = 8.0

LOG2E = 1.4426950408889634
NEG = -1e30
VMEM_LIMIT = 56 * 1024 * 1024

IN_TILE_EVEN = 512
IN_TILE_ODD = 512
ATTN_TILE = 512
LRU_TILE = 512
LRU_SUB = LRU_TILE // SUBLANES
LRU_PITCH = LRU_SUB + SUBLANES
LRU_HALF_GROUPS = 8


def _dot(a, b):
    return jnp.dot(a, b, preferred_element_type=F32)


def _split3(v):
    hi = v.astype(BF16)
    r1 = v - hi.astype(F32)
    mid = r1.astype(BF16)
    lo = (r1 - mid.astype(F32)).astype(BF16)
    return hi, mid, lo


def _dot3(*ops, rhs_split=True):
    if rhs_split:
        m, parts = ops[0], ops[1:]
        return _dot(m, parts[0]) + _dot(m, parts[1]) + _dot(m, parts[2])
    parts, m = ops[:3], ops[3]
    return _dot(parts[0], m) + _dot(parts[1], m) + _dot(parts[2], m)


def _rms(x, g):
    ms = jnp.mean(x * x, axis=-1, keepdims=True)
    return x * lax.rsqrt(ms + EPS) * g


def _sigmoid(x):
    return 0.5 * jnp.tanh(0.5 * x) + 0.5


def _silu(x):
    h = 0.5 * x
    return h + h * jnp.tanh(h)


def _params(n_axes=1):
    return pltpu.CompilerParams(dimension_semantics=("arbitrary",) * n_axes,
                                vmem_limit_bytes=VMEM_LIMIT)


def _const_spec(shape):
    nd = len(shape)
    return pl.BlockSpec(shape, lambda i: (0,) * nd, pipeline_mode=pl.Buffered(1))


def _row_spec(tm, width, rev_n=None):
    if rev_n is None:
        return pl.BlockSpec((tm, width), lambda i: (i, 0))
    return pl.BlockSpec((tm, width), lambda i: (rev_n - 1 - i, 0))


def _halo_specs(tm, n_rows):
    hb = tm // HALO
    nh = n_rows // HALO
    prev = pl.BlockSpec((HALO, D_MODEL), lambda i: (jnp.maximum(i * hb - 1, 0), 0))
    main = pl.BlockSpec((tm, D_MODEL), lambda i: (i, 0))
    nxt = pl.BlockSpec((HALO, D_MODEL), lambda i: (jnp.minimum((i + 1) * hb, nh - 1), 0))
    return [prev, main, nxt]


def _fill_normed(xp_ref, xm_ref, xn_ref, g_ref, h_scr, tm):
    i = pl.program_id(0)
    n = pl.num_programs(0)
    g = g_ref[...]
    hp = _rms(xp_ref[...], g)
    hn = _rms(xn_ref[...], g)
    h_scr[0:HALO, :] = jnp.where(i > 0, hp, 0.0).astype(BF16)
    h_scr[HALO:HALO + tm, :] = _rms(xm_ref[...], g).astype(BF16)
    h_scr[HALO + tm:HALO + tm + HALO, :] = jnp.where(i < n - 1, hn, 0.0).astype(BF16)


def _proj_conv(h_ext, w_ref, u_scr, cw_ref, cb_ref, tm, c0, width):
    u = _dot(h_ext, w_ref[:, c0:c0 + width])
    base = HALO - CONV_LEFT
    outs = []
    for j in range(width // LANES):
        slab = c0 // LANES + j
        u_scr[slab] = u[:, j * LANES:(j + 1) * LANES]
        cs_ = slice(c0 + j * LANES, c0 + (j + 1) * LANES)
        acc = cb_ref[:, cs_] + cw_ref[0:1, cs_] * u_scr[slab, base:base + tm, :]
        for k in range(1, CONV_W):
            acc = acc + cw_ref[k:k + 1, cs_] * u_scr[slab, base + k:base + k + tm, :]
        outs.append(acc)
    return outs


def _even_in_body(xp_ref, xm_ref, xn_ref, g_ref, wz_ref, wx_ref, wdt_ref, wq_ref, wk_ref, wv_ref,
                  wg_ref, cw_ref, cb_ref, dtb_ref, cos_ref, sa_ref, sb_ref,
                  z_ref, xc_ref, b_ref, c_ref, dt_ref, q_ref, kz_ref, vz_ref, sg_ref,
                  h_scr, u_scr, *, tm):
    _fill_normed(xp_ref, xm_ref, xn_ref, g_ref, h_scr, tm)
    h_ext = h_scr[...]
    hm = h_scr[HALO:HALO + tm, :]
    cstep = 256
    cos_t = cos_ref[...]
    sin_a = sa_ref[...]
    sin_b = sb_ref[...]
    low = lax.broadcasted_iota(jnp.int32, (tm, LANES), 1) < ATTN_HEAD_DIM

    def rope(t):
        return (t * cos_t + pltpu.roll(t, LANES - ROPE_DIM // 2, 1) * sin_a
                + pltpu.roll(t, ROPE_DIM // 2, 1) * sin_b)

    def do_xbc(c0):
        outs = _proj_conv(h_ext, wx_ref, u_scr, cw_ref, cb_ref, tm, c0, cstep)
        for j, acc in enumerate(outs):
            y = _silu(acc)
            c = c0 + j * LANES
            if c < SSD_INNER:
                xc_ref[:, c:c + LANES] = y
            elif c < SSD_INNER + 256:
                b_ref[:, c - SSD_INNER:c - SSD_INNER + LANES] = y
            else:
                c_ref[:, c - SSD_INNER - 256:c - SSD_INNER - 256 + LANES] = y

    def do_z(c0):
        z_ref[:, c0:c0 + cstep] = _silu(_dot(hm, wz_ref[:, c0:c0 + cstep])).astype(BF16)

    def do_gate(c0):
        sg_ref[:, c0:c0 + cstep] = _silu(_dot(hm, wg_ref[:, c0:c0 + cstep])).astype(BF16)

    def do_q(c0):
        qf = _dot(hm, wq_ref[:, c0:c0 + cstep]) * (ATTN_HEAD_DIM ** -0.5)
        for j in range(cstep // LANES):
            q_ref[:, c0 + j * LANES:c0 + (j + 1) * LANES] = rope(qf[:, j * LANES:(j + 1) * LANES]).astype(BF16)

    def do_kv(w_ref, dst, rotary):
        f = _dot(hm, w_ref[...])
        for j in range(2):
            a = f[:, j * LANES:(j + 1) * LANES]
            if rotary:
                a = rope(a)
            r = pltpu.roll(a, ATTN_HEAD_DIM, 1)
            c = 4 * j * LANES
            dst[:, c:c + LANES] = jnp.where(low, a, 0.0).astype(BF16)
            dst[:, c + LANES:c + 2 * LANES] = jnp.where(low, 0.0, r).astype(BF16)
            dst[:, c + 2 * LANES:c + 3 * LANES] = jnp.where(low, r, 0.0).astype(BF16)
            dst[:, c + 3 * LANES:c + 4 * LANES] = jnp.where(low, 0.0, a).astype(BF16)

    for n in range(4):
        do_xbc(n * cstep)
        do_z(n * cstep)
        do_q(n * cstep)
        do_gate(n * cstep)
        if n == 0:
            do_xbc(4 * cstep)
        elif n == 1:
            do_xbc(5 * cstep)
        elif n == 2:
            do_kv(wk_ref, kz_ref, True)
        else:
            do_kv(wv_ref, vz_ref, False)
    dt_ref[...] = jax.nn.softplus(_dot(hm, wdt_ref[...]) + dtb_ref[...])


def _even_in(x, g, wz, wx, wdt, wq, wk, wv, wg, cw, cb, dtb, cos_t, sin_a, sin_b):
    n_rows = x.shape[0]
    tm = IN_TILE_EVEN
    rows = functools.partial(_row_spec, tm)
    in_specs = _halo_specs(tm, n_rows) + [
        _const_spec((1, D_MODEL)),
        _const_spec(wz.shape), _const_spec(wx.shape), _const_spec(wdt.shape), _const_spec(wq.shape),
        _const_spec(wk.shape), _const_spec(wv.shape), _const_spec(wg.shape),
        _const_spec(cw.shape), _const_spec(cb.shape), _const_spec(dtb.shape),
        rows(LANES), rows(LANES), rows(LANES),
    ]
    out_shape = (
        jax.ShapeDtypeStruct((n_rows, SSD_INNER), BF16),
        jax.ShapeDtypeStruct((n_rows, SSD_INNER), F32),
        jax.ShapeDtypeStruct((n_rows, 256), F32),
        jax.ShapeDtypeStruct((n_rows, 256), F32),
        jax.ShapeDtypeStruct((n_rows, LANES), F32),
        jax.ShapeDtypeStruct((n_rows, D_MODEL), BF16),
        jax.ShapeDtypeStruct((n_rows, D_MODEL), BF16),
        jax.ShapeDtypeStruct((n_rows, D_MODEL), BF16),
        jax.ShapeDtypeStruct((n_rows, D_MODEL), BF16),
    )
    out_specs = (rows(SSD_INNER), rows(SSD_INNER), rows(256), rows(256), rows(LANES),
                 rows(D_MODEL), rows(D_MODEL), rows(D_MODEL), rows(D_MODEL))
    return pl.pallas_call(
        functools.partial(_even_in_body, tm=tm),
        grid=(n_rows // tm,),
        in_specs=in_specs, out_specs=out_specs, out_shape=out_shape,
        scratch_shapes=[pltpu.VMEM((tm + 2 * HALO, D_MODEL), BF16),
                        pltpu.VMEM((SSD_CONV_CH // LANES, tm + 2 * HALO, LANES), F32)],
        compiler_params=_params(), name="even_in",
    )(x, x, x, g, wz, wx, wdt, wq, wk, wv, wg, cw, cb, dtb, cos_t, sin_a, sin_b)


def _ssd_body(*refs, reverse, combine):
    if combine:
        (xc_ref, b_ref, c_ref, dt_ref, alog_ref, e_ref, z_ref, yb_ref, dsk_ref, nw_ref,
         y_ref, state, cb_scr, yoff_scr, ytmp) = refs
    else:
        xc_ref, b_ref, c_ref, dt_ref, alog_ref, e_ref, y_ref, state, cb_scr, yoff_scr = refs
        z_ref = yb_ref = dsk_ref = nw_ref = ytmp = None

    @pl.when(pl.program_id(0) == 0)
    def _():
        state[...] = jnp.zeros_like(state)

    chunks = list(range(SSD_CHUNKS_PER_STEP))
    if reverse:
        chunks = chunks[::-1]
    for ci in chunks:
        rs = pl.ds(ci * SSD_CHUNK, SSD_CHUNK)
        y_view = y_ref.at[rs]
        _ssd_chunk(xc_ref.at[rs], b_ref.at[rs], c_ref.at[rs], dt_ref.at[rs], alog_ref, e_ref,
                   z_ref.at[rs] if combine else None, yb_ref.at[rs] if combine else None,
                   dsk_ref, nw_ref, y_view, state, cb_scr, yoff_scr,
                   ytmp if combine else y_view, reverse=reverse, combine=combine)


def _ssd_chunk(xc_ref, b_ref, c_ref, dt_ref, alog_ref, e_ref, z_ref, yb_ref, dsk_ref, nw_ref,
               y_ref, state, cb_scr, yoff_scr, ytmp, *, reverse, combine):
    t_len = SSD_CHUNK
    off = SSD_HEADS if reverse else 0

    row = lax.broadcasted_iota(jnp.int32, (t_len, t_len), 0)
    col = lax.broadcasted_iota(jnp.int32, (t_len, t_len), 1)
    mask = (col >= row) if reverse else (col <= row)
    tri = jnp.where(mask, 1.0, 0.0).astype(BF16)

    lane = lax.broadcasted_iota(jnp.int32, (1, LANES), 1)
    in_dir = (lane >= off) & (lane < off + SSD_HEADS)
    a_rate = jnp.where(in_dir, -LOG2E * jnp.exp(alog_ref[...]), 0.0)

    dt = dt_ref[...]
    cs = _dot3(tri, *_split3(dt * a_rate))
    log_dt = jnp.log2(dt)
    cs_t = (cs - log_dt).T
    last = 0 if reverse else t_len - 1
    cs_last = cs[last:last + 1, :]
    w_hi, w_mid, _ = _split3(jnp.exp2(cs_last - cs + log_dt))
    expand = e_ref[...]
    wdt_x = _dot(w_hi, expand) + _dot(w_mid, expand)

    b_f = b_ref[...]
    c_b = c_ref[...].astype(BF16)
    b_b = b_f.astype(BF16)
    b_t = b_f.T.astype(BF16)
    low = lax.broadcasted_iota(jnp.int32, (t_len, LANES), 1) < SSD_HEAD_DIM

    hb = t_len // 2
    key_blocks = ((0, 1), (1,)) if reverse else ((0,), (0, 1))
    diag_mask = mask[0:hb, 0:hb]
    low_hb = lax.broadcasted_iota(jnp.int32, (hb, LANES), 1) < SSD_HEAD_DIM

    gw = SSD_INNER // SSD_GROUPS
    for g in range(SSD_GROUPS):
        c_g = c_b[:, g * SSD_STATE:(g + 1) * SSD_STATE]
        b_g = b_b[:, g * SSD_STATE:(g + 1) * SSD_STATE]
        cb_scr[...] = lax.dot_general(c_g, b_g, (((1,), (1,)), ((), ())), preferred_element_type=F32)
        s_g = state[:, g * gw:(g + 1) * gw]
        yoff_scr[...] = _dot(c_g, s_g.astype(BF16))
        edge_decay = []
        for tt in range(gw // LANES):
            t = g * (gw // LANES) + tt
            ts = slice(t * LANES, (t + 1) * LANES)
            xt = xc_ref[:, ts]
            x_halves = (jnp.where(low, xt, 0.0).astype(BF16), jnp.where(low, 0.0, xt).astype(BF16))
            for rb in range(2):
                rs = slice(rb * hb, (rb + 1) * hb)
                cols = [jnp.broadcast_to(cs[rs, off + 2 * t + half:off + 2 * t + half + 1], (hb, LANES))
                        for half in range(2)]
                e_tile = jnp.where(low_hb, jnp.exp2(cols[0]), jnp.exp2(cols[1]))
                if rb * hb <= last < (rb + 1) * hb:
                    edge_decay.append(e_tile[last - rb * hb:last - rb * hb + 1, :])
                acc = yoff_scr[rs, tt * LANES:(tt + 1) * LANES] * e_tile
                for half in range(2):
                    ln = off + 2 * t + half
                    pieces = []
                    for kb in key_blocks[rb]:
                        ks = slice(kb * hb, (kb + 1) * hb)
                        seg = cols[half] - cs_t[ln:ln + 1, ks]
                        if kb == rb:
                            seg = jnp.where(diag_mask, seg, NEG)
                        pieces.append((cb_scr[rs, ks] * jnp.exp2(seg)).astype(BF16))
                    m = pieces[0] if len(pieces) == 1 else jnp.concatenate(pieces, axis=1)
                    k0, k1 = key_blocks[rb][0] * hb, (key_blocks[rb][-1] + 1) * hb
                    acc = acc + _dot(m, x_halves[half][k0:k1, :])
                ytmp[rs, ts] = acc.astype(ytmp.dtype)
        gs = slice(g * gw, (g + 1) * gw)
        xw_b = (xc_ref[:, gs] * wdt_x[:, gs]).astype(BF16)
        state[:, gs] = (jnp.concatenate(edge_decay, axis=1) * s_g
                        + _dot(b_t[g * SSD_STATE:(g + 1) * SSD_STATE, :], xw_b))

    if combine:
        y = ytmp[...] + yb_ref[...].astype(F32) + xc_ref[...] * dsk_ref[...]
        y = y * z_ref[...].astype(F32)
        for g in range(SSD_GROUPS):
            yg = y[:, g * gw:(g + 1) * gw]
            ms = jnp.mean(yg * yg, axis=-1, keepdims=True)
            y_ref[:, g * gw:(g + 1) * gw] = (
                yg * lax.rsqrt(ms + EPS) * nw_ref[:, g * gw:(g + 1) * gw]).astype(BF16)


def _ssd(xc, bm, cm, dt, alog_row, expand, reverse, combine_args=None):
    n_rows = xc.shape[0]
    t_len = SSD_CHUNK
    tile = SSD_CHUNKS_PER_STEP * t_len
    nc = n_rows // tile
    rows = functools.partial(_row_spec, tile, rev_n=nc if reverse else None)
    combine = combine_args is not None
    in_specs = [rows(SSD_INNER), rows(256), rows(256), rows(LANES),
                _const_spec((1, LANES)), _const_spec((LANES, SSD_INNER))]
    args = [xc, bm, cm, dt, alog_row, expand]
    scratch = [pltpu.VMEM((SSD_STATE, SSD_INNER), F32),
               pltpu.VMEM((t_len, t_len), F32),
               pltpu.VMEM((t_len, SSD_INNER // SSD_GROUPS), F32)]
    if combine:
        z, y_bwd, dskip_x, norm_w = combine_args
        in_specs += [rows(SSD_INNER), rows(SSD_INNER),
                     _const_spec((1, SSD_INNER)), _const_spec((1, SSD_INNER))]
        args += [z, y_bwd, dskip_x, norm_w]
        scratch.append(pltpu.VMEM((t_len, SSD_INNER), F32))
    out_dtype = BF16
    return pl.pallas_call(
        functools.partial(_ssd_body, reverse=reverse, combine=combine),
        grid=(nc,),
        in_specs=in_specs, out_specs=rows(SSD_INNER),
        out_shape=jax.ShapeDtypeStruct((n_rows, SSD_INNER), out_dtype),
        scratch_shapes=scratch,
        compiler_params=_params(), name="ssd_fwd" if combine else "ssd_bwd",
    )(*args)


def _attn_body(sink_ref, q_ref, kp_ref, km_ref, kn_ref, vp_ref, vm_ref, vn_ref, sg_ref, ys_ref,
               x_ref, w1_ref, w2_ref, o_ref, k_scr, v_scr, y_scr, yprev_scr, s_scr, *, tq):
    i = pl.program_id(0)
    n = pl.num_programs(0) - 1
    tile = jnp.minimum(i, n - 1)
    blk = ATTN_BLOCK
    nq = tq // blk
    wc = D_MODEL // nq

    @pl.when(i == 0)
    def _():
        y_scr[...] = jnp.zeros_like(y_scr)

    yprev_scr[...] = y_scr[...]
    for scr, p_ref, m_ref, n_ref in ((k_scr, kp_ref, km_ref, kn_ref), (v_scr, vp_ref, vm_ref, vn_ref)):
        scr[0:blk, :] = p_ref[...]
        scr[blk:blk + tq, :] = m_ref[...]
        scr[blk + tq:blk + tq + blk, :] = n_ref[...]

    row2 = lax.broadcasted_iota(jnp.int32, (2 * blk, blk), 0)
    qi = row2 & (blk - 1)
    kj = lax.broadcasted_iota(jnp.int32, (2 * blk, blk), 1)
    first_tile = row2[:, 0:1] < blk
    n_kv = ATTN_KV_HEADS

    for b in range(nq):
        r0 = b * blk
        gb = tile * nq + b
        bias_l = jnp.where((kj >= qi) & (gb > 0), 0.0, NEG)
        bias_r = jnp.where((kj <= qi) & (gb < n * nq - 1), 0.0, NEG)
        for hk in range(n_kv):
            q2 = jnp.concatenate(
                [q_ref[pl.ds(r0, blk), (2 * hk) * LANES:(2 * hk + 1) * LANES],
                 q_ref[pl.ds(r0, blk), (2 * hk + 1) * LANES:(2 * hk + 2) * LANES]], axis=0)
            kz = jnp.concatenate(
                [k_scr[pl.ds(r0, 3 * blk), (2 * hk + half) * LANES:(2 * hk + half + 1) * LANES]
                 for half in range(2)], axis=0)
            s_scr[hk] = lax.dot_general(q2, kz, (((1,), (1,)), ((), ())), preferred_element_type=F32)
        oc = slice(b * wc, (b + 1) * wc)
        o_ref[:, oc] = (x_ref[:, oc] + _dot(ys_ref[...], w1_ref[:, oc])
                        + _dot(yprev_scr[...], w2_ref[:, oc]))
        for hk in range(n_kv):
            out = None
            for half in range(2):
                c = (2 * hk + half) * LANES
                k0 = half * 3 * blk
                s_l = s_scr[hk, :, k0:k0 + blk] + bias_l
                s_m = s_scr[hk, :, k0 + blk:k0 + 2 * blk]
                s_r = s_scr[hk, :, k0 + 2 * blk:k0 + 3 * blk] + bias_r
                sink = jnp.where(first_tile, sink_ref[4 * hk + half], sink_ref[4 * hk + 2 + half])
                m = jnp.max(jnp.maximum(jnp.maximum(s_l, s_m), s_r), axis=-1, keepdims=True)
                m = jnp.maximum(m, sink)
                p_l = jnp.exp(s_l - m)
                p_m = jnp.exp(s_m - m)
                p_r = jnp.exp(s_r - m)
                den = jnp.sum(p_l + p_m + p_r, axis=-1, keepdims=True) + jnp.exp(sink - m)
                p = jnp.concatenate([p_l, p_m, p_r], axis=1).astype(BF16)
                o = _dot(p, v_scr[pl.ds(r0, 3 * blk), c:c + LANES]) * (1.0 / den)
                out = o if out is None else out + o
            for k in range(2):
                t = 2 * hk + k
                y_scr[pl.ds(r0, blk), t * LANES:(t + 1) * LANES] = (
                    out[k * blk:(k + 1) * blk, :]
                    * sg_ref[pl.ds(r0, blk), t * LANES:(t + 1) * LANES].astype(F32)).astype(BF16)


def _attn_out(sink, q, kz, vz, sg, y_ssd, x, w1, w2):
    n_rows = x.shape[0]
    tq = ATTN_TILE
    blk = ATTN_BLOCK
    per = tq // blk
    nb = n_rows // blk
    n = n_rows // tq

    def cur(i):
        return jnp.minimum(i, n - 1)

    def lag(i):
        return jnp.maximum(i - 1, 0)

    rows_cur = pl.BlockSpec((tq, D_MODEL), lambda i: (cur(i), 0))
    rows_lag = pl.BlockSpec((tq, D_MODEL), lambda i: (lag(i), 0))
    prev = pl.BlockSpec((blk, D_MODEL), lambda i: (jnp.maximum(cur(i) * per - 1, 0), 0))
    nxt = pl.BlockSpec((blk, D_MODEL), lambda i: (jnp.minimum((cur(i) + 1) * per, nb - 1), 0))
    in_specs = [pl.BlockSpec(memory_space=pltpu.SMEM),
                rows_cur, prev, rows_cur, nxt, prev, rows_cur, nxt,
                rows_cur, rows_lag, rows_lag,
                _const_spec(w1.shape), _const_spec(w2.shape)]
    return pl.pallas_call(
        functools.partial(_attn_body, tq=tq),
        grid=(n + 1,),
        in_specs=in_specs, out_specs=rows_lag,
        out_shape=jax.ShapeDtypeStruct((n_rows, D_MODEL), F32),
        scratch_shapes=[pltpu.VMEM((tq + 2 * blk, D_MODEL), BF16),
                        pltpu.VMEM((tq + 2 * blk, D_MODEL), BF16),
                        pltpu.VMEM((tq, D_MODEL), BF16),
                        pltpu.VMEM((tq, D_MODEL), BF16),
                        pltpu.VMEM((ATTN_KV_HEADS, 2 * blk, 6 * blk), F32)],
        compiler_params=_params(), name="attn_out",
    )(sink, q, kz, kz, kz, vz, vz, vz, sg, y_ssd, x, w1, w2)


def _odd_in_body(xp_ref, xm_ref, xn_ref, g_ref, wx_ref, wg_ref, cw_ref, cb_ref,
                 xc_ref, sg_ref, h_scr, u_scr, *, tm):
    _fill_normed(xp_ref, xm_ref, xn_ref, g_ref, h_scr, tm)
    h_ext = h_scr[...]
    hm = h_scr[HALO:HALO + tm, :]
    cstep = 256
    for c0 in range(0, LRU_WIDTH, cstep):
        outs = _proj_conv(h_ext, wx_ref, u_scr, cw_ref, cb_ref, tm, c0, cstep)
        for j, acc in enumerate(outs):
            xc_ref[:, c0 + j * LANES:c0 + (j + 1) * LANES] = acc
        sg_ref[:, c0:c0 + cstep] = _silu(_dot(hm, wg_ref[:, c0:c0 + cstep])).astype(BF16)


def _odd_in(x, g, wx, wg, cw, cb):
    n_rows = x.shape[0]
    tm = IN_TILE_ODD
    rows = functools.partial(_row_spec, tm)
    in_specs = _halo_specs(tm, n_rows) + [
        _const_spec((1, D_MODEL)), _const_spec(wx.shape), _const_spec(wg.shape),
        _const_spec(cw.shape), _const_spec(cb.shape)]
    return pl.pallas_call(
        functools.partial(_odd_in_body, tm=tm),
        grid=(n_rows // tm,),
        in_specs=in_specs, out_specs=(rows(LRU_WIDTH), rows(LRU_WIDTH)),
        out_shape=(jax.ShapeDtypeStruct((n_rows, LRU_WIDTH), F32),
                   jax.ShapeDtypeStruct((n_rows, LRU_WIDTH), BF16)),
        scratch_shapes=[pltpu.VMEM((tm + 2 * HALO, D_MODEL), BF16),
                        pltpu.VMEM((LRU_WIDTH // LANES, tm + 2 * HALO, LANES), F32)],
        compiler_params=_params(), name="odd_in",
    )(x, x, x, g, wx, wg, cw, cb)


def _lru_body(*refs, reverse, combine, final):
    if combine:
        if final:
            (xc_ref, w_ref, ba_ref, bx_ref, lam_ref, hb_ref, sg_ref, x_ref, wo_ref, fw_ref,
             o_ref, a_scr, u_scr, carry, n_scr, y_scr) = refs
        else:
            (xc_ref, w_ref, ba_ref, bx_ref, lam_ref, hb_ref, sg_ref, x_ref, wo_ref,
             o_ref, a_scr, u_scr, carry, n_scr, y_scr) = refs
    else:
        xc_ref, w_ref, ba_ref, bx_ref, lam_ref, o_ref, a_scr, u_scr, carry = refs
    tm, sub, pitch = LRU_TILE, LRU_SUB, LRU_PITCH
    vregs_per_sub = sub // SUBLANES
    grp_per_blk = LRU_BLOCK // LANES
    half_w = LRU_HALF_GROUPS * LANES

    @pl.when(pl.program_id(0) == 0)
    def _():
        carry[...] = jnp.zeros_like(carry)

    half_rate = (-0.5 * LRU_C) * jax.nn.softplus(-lam_ref[...])
    steps = list(range(sub))
    order = list(range(SUBLANES))
    if reverse:
        steps = steps[::-1]
        order = order[::-1]

    for blk in range(LRU_BLOCKS):
        c0 = blk * LRU_BLOCK
        xb = xc_ref[:, c0:c0 + LRU_BLOCK]
        ri = _dot(xb.astype(BF16), w_ref[blk])
        hr = half_rate[:, c0:c0 + LRU_BLOCK]
        log_a = hr + hr * jnp.tanh(0.5 * (ri[:, :LRU_BLOCK] + ba_ref[:, c0:c0 + LRU_BLOCK]))
        hx = 0.5 * xb
        gx = hx + hx * jnp.tanh(0.5 * (ri[:, LRU_BLOCK:] + bx_ref[:, c0:c0 + LRU_BLOCK]))
        a = jnp.exp(log_a)
        th = jnp.tanh(log_a)
        num = -2.0 * th
        root = jnp.where(num > 0.0, num * lax.rsqrt(num), 0.0) * lax.rsqrt(1.0 - th)
        u = root * gx
        for half in range(grp_per_blk):
            g = blk * grp_per_blk + half
            ls = slice(half * LANES, (half + 1) * LANES)
            for k in range(tm // SUBLANES):
                s = k // vregs_per_sub
                jj = (k % vregs_per_sub) * SUBLANES
                dst = pl.ds(jj * SUBLANES + s, SUBLANES, stride=SUBLANES)
                a_scr[g, dst, :] = a[k * SUBLANES:(k + 1) * SUBLANES, ls]
                u_scr[g, dst, :] = u[k * SUBLANES:(k + 1) * SUBLANES, ls]

    def rows_t(j):
        return slice(j * SUBLANES, (j + 1) * SUBLANES)

    for hf in range(LRU_WIDTH // half_w):
        gs = slice(hf * LRU_HALF_GROUPS, (hf + 1) * LRU_HALF_GROUPS)
        h = u_scr[gs, rows_t(steps[0]), :]
        p = a_scr[gs, rows_t(steps[0]), :]
        for j in steps[1:]:
            aj = a_scr[gs, rows_t(j), :]
            h = aj * h + u_scr[gs, rows_t(j), :]
            p = aj * p
            u_scr[gs, rows_t(j), :] = h
            a_scr[gs, rows_t(j), :] = p

        c = carry[gs]
        carry_in = [None] * SUBLANES
        for s in order:
            carry_in[s] = c
            c = p[:, s:s + 1, :] * c + h[:, s:s + 1, :]
        carry[gs] = c
        cmat = jnp.concatenate(carry_in, axis=1)

        for j in range(sub):
            hj = u_scr[gs, rows_t(j), :] + a_scr[gs, rows_t(j), :] * cmat
            for g in range(LRU_HALF_GROUPS):
                cs_ = slice(hf * half_w + g * LANES, hf * half_w + (g + 1) * LANES)
                if combine:
                    n_scr[g, pl.ds(j, SUBLANES, stride=pitch), :] = hj[g] + hb_ref[rows_t(j), cs_]
                else:
                    o_ref[rows_t(j), cs_] = hj[g]

        if combine:
            for s in range(SUBLANES):
                rs = slice(s * sub, (s + 1) * sub)
                for g in range(LRU_HALF_GROUPS):
                    cs_ = slice(hf * half_w + g * LANES, hf * half_w + (g + 1) * LANES)
                    y_scr[rs, cs_] = (n_scr[g, s * pitch:s * pitch + sub, :]
                                      * sg_ref[rs, cs_].astype(F32)).astype(BF16)

    if combine:
        out = x_ref[...] + _dot(y_scr[...], wo_ref[...])
        if final:
            out = _rms(out, fw_ref[...])
        o_ref[...] = out


def _lru(xc, w_gates, b_a, b_x, lam, reverse, combine_args=None, final_w=None):
    n_rows = xc.shape[0]
    tm = LRU_TILE
    n = n_rows // tm
    combine = combine_args is not None
    final = final_w is not None
    rows = functools.partial(_row_spec, tm, rev_n=n if reverse else None)
    in_specs = [rows(LRU_WIDTH), _const_spec(w_gates.shape), _const_spec((1, LRU_WIDTH)),
                _const_spec((1, LRU_WIDTH)), _const_spec((1, LRU_WIDTH))]
    args = [xc, w_gates, b_a, b_x, lam]
    n_grp = LRU_WIDTH // LANES
    scratch = [pltpu.VMEM((n_grp, tm, LANES), F32),
               pltpu.VMEM((n_grp, tm, LANES), F32),
               pltpu.VMEM((n_grp, 1, LANES), F32)]
    if combine:
        h_bwd, sg, x, w_out = combine_args
        in_specs += [rows(LRU_WIDTH), rows(LRU_WIDTH), rows(D_MODEL), _const_spec(w_out.shape)]
        args += [h_bwd, sg, x, w_out]
        if final:
            in_specs.append(_const_spec((1, D_MODEL)))
            args.append(final_w)
        scratch.append(pltpu.VMEM((LRU_HALF_GROUPS, SUBLANES * LRU_PITCH, LANES), F32))
        scratch.append(pltpu.VMEM((tm, LRU_WIDTH), BF16))
        out_w = D_MODEL
    else:
        out_w = LRU_WIDTH
    return pl.pallas_call(
        functools.partial(_lru_body, reverse=reverse, combine=combine, final=final),
        grid=(n,),
        in_specs=in_specs, out_specs=rows(out_w),
        out_shape=jax.ShapeDtypeStruct((n_rows, out_w), F32),
        scratch_shapes=scratch,
        compiler_params=_params(), name="lru_fwd" if combine else "lru_bwd",
    )(*args)


def _rope_tables(n_rows):
    inv = ROPE_THETA ** (-jnp.arange(0, ROPE_DIM, 2, dtype=F32) / ROPE_DIM)
    ang = jnp.arange(n_rows, dtype=F32)[:, None] * inv[None, :]
    cos, sin = jnp.cos(ang), jnp.sin(ang)
    half = ROPE_DIM // 2
    ones = jnp.ones((n_rows, ATTN_HEAD_DIM - ROPE_DIM), F32)
    zeros = jnp.zeros((n_rows, ATTN_HEAD_DIM - ROPE_DIM), F32)
    zero_h = jnp.zeros((n_rows, half), F32)
    reps = LANES // ATTN_HEAD_DIM
    cos_t = jnp.tile(jnp.concatenate([cos, cos, ones], axis=1), (1, reps))
    sin_a = jnp.tile(jnp.concatenate([-sin, zero_h, zeros], axis=1), (1, reps))
    sin_b = jnp.tile(jnp.concatenate([zero_h, sin, zeros], axis=1), (1, reps))
    return cos_t, sin_a, sin_b


def _pad_lanes(v):
    return jnp.pad(v.astype(F32), (0, LANES - v.shape[0]))[None, :]


def _even_layer(x, g_norm, w_in, conv_w, conv_b, dt_bias, a_log, d_skip, ssd_norm_w, sink, w_out, rope):
    sizes = (SSD_INNER, SSD_CONV_CH, 2 * SSD_HEADS, D_MODEL, 256, 256, D_MODEL)
    cuts = [0]
    for s in sizes:
        cuts.append(cuts[-1] + s)
    wz, wx, wdt, wq, wk, wv, wg = (w_in[:, cuts[k]:cuts[k + 1]].astype(BF16) for k in range(7))
    wdt = jnp.pad(wdt, ((0, 0), (0, LANES - 2 * SSD_HEADS)))
    dtb = _pad_lanes(dt_bias.reshape(-1))
    z, xc, bm, cm, dt, q, kz, vz, sg = _even_in(
        x, g_norm[None, :], wz, wx, wdt, wq, wk, wv, wg, conv_w, conv_b[None, :], dtb, *rope)

    lane = jnp.arange(LANES)[:, None]
    head = (jnp.arange(SSD_INNER) // SSD_HEAD_DIM)[None, :]
    alog_row = _pad_lanes(a_log.reshape(-1))
    y_bwd = _ssd(xc, bm, cm, dt, alog_row, (lane == head + SSD_HEADS).astype(BF16), reverse=True)
    dskip_x = jnp.repeat(d_skip.astype(F32), SSD_HEAD_DIM)[None, :]
    y_ssd = _ssd(xc, bm, cm, dt, alog_row, (lane == head).astype(BF16), reverse=False,
                 combine_args=(z, y_bwd, dskip_x, ssd_norm_w[None, :]))

    w_out_b = w_out.astype(BF16)
    return _attn_out(sink.astype(F32), q, kz, vz, sg, y_ssd, x, w_out_b[:SSD_INNER], w_out_b[SSD_INNER:])


def _odd_layer(x, g_norm, w_in, conv_w, conv_b, w_a, b_a, w_x, b_x, lam, w_out, final_w):
    w_in_b = w_in.astype(BF16)
    xc, sg = _odd_in(x, g_norm[None, :], w_in_b[:, :LRU_WIDTH], w_in_b[:, LRU_WIDTH:],
                     conv_w, conv_b[None, :])
    w_gates = jnp.concatenate([w_a, w_x], axis=-1).astype(BF16)
    h_bwd = _lru(xc, w_gates[1], b_a[1][None, :], b_x[1][None, :], lam[1][None, :], reverse=True)
    return _lru(xc, w_gates[0], b_a[0][None, :], b_x[0][None, :], lam[0][None, :], reverse=False,
                combine_args=(h_bwd, sg, x, w_out.astype(BF16)),
                final_w=None if final_w is None else final_w[None, :])


def kernel(x, norm_w, final_norm_w, ev_w_in, ev_conv_w, ev_conv_b, ev_dt_bias, ev_a_log, ev_d_skip, ev_ssd_norm_w, ev_sink, ev_w_out, od_w_in, od_conv_w, od_conv_b, od_w_a, od_b_a, od_w_x, od_b_x, od_lambda, od_w_out):
    batch, n_rows, _ = x.shape
    depth = norm_w.shape[0]
    assert batch == 1 and depth % 2 == 0
    rope = _rope_tables(n_rows)
    h = x[0]
    for layer in range(depth):
        j = layer // 2
        if layer % 2 == 0:
            h = _even_layer(h, norm_w[layer], ev_w_in[j], ev_conv_w[j], ev_conv_b[j], ev_dt_bias[j],
                            ev_a_log[j], ev_d_skip[j], ev_ssd_norm_w[j], ev_sink[j], ev_w_out[j], rope)
        else:
            h = _odd_layer(h, norm_w[layer], od_w_in[j], od_conv_w[j], od_conv_b[j], od_w_a[j],
                           od_b_a[j], od_w_x[j], od_b_x[j], od_lambda[j], od_w_out[j],
                           final_norm_w if layer == depth - 1 else None)
    return h[None]
```

```python
import functools

import jax
import jax.numpy as jnp
from jax import lax
from jax.experimental import pallas as pl
from jax.experimental.pallas import tpu as pltpu

F32 = jnp.float32
BF16 = jnp.bfloat16

D_MODEL = 1024
EPS = 1e-6
LANES = 128
SUBLANES = 8
HALO = 16
CONV_W = 4
CONV_LEFT = 2

SSD_HEADS = 16
SSD_HEAD_DIM = 64
SSD_STATE = 128
SSD_GROUPS = 2
SSD_CHUNK = 256
SSD_CHUNKS_PER_STEP = 2
SSD_INNER = 1024
SSD_CONV_CH = 1536

EVEN_W_SIZES = (1024, 1536, 1024, 256, 256, 1024, 128)

ATTN_HEADS = 16
ATTN_KV_HEADS = 4
ATTN_HEAD_DIM = 64
ATTN_BLOCK = 128
ROPE_THETA = 500000.0
ROPE_DIM = 16

LRU_WIDTH = 2048
LRU_BLOCK = 256
LRU_BLOCKS = 8
LRU_C = 8.0

LOG2E = 1.4426950408889634
NEG = -1e30
VMEM_LIMIT = 56 * 1024 * 1024

IN_TILE_EVEN = 512
IN_TILE_ODD = 512
ATTN_TILE = 512
LRU_TILE = 512
LRU_SUB = LRU_TILE // SUBLANES
LRU_PITCH = LRU_SUB + SUBLANES
LRU_HALF_GROUPS = 8


def _dot(a, b):
    return jnp.dot(a, b, preferred_element_type=F32)


def _split3(v):
    hi = v.astype(BF16)
    r1 = v - hi.astype(F32)
    mid = r1.astype(BF16)
    lo = (r1 - mid.astype(F32)).astype(BF16)
    return hi, mid, lo


def _dot3(*ops, rhs_split=True):
    if rhs_split:
        m, parts = ops[0], ops[1:]
        return _dot(m, parts[0]) + _dot(m, parts[1]) + _dot(m, parts[2])
    parts, m = ops[:3], ops[3]
    return _dot(parts[0], m) + _dot(parts[1], m) + _dot(parts[2], m)


def _rms(x, g):
    ms = jnp.mean(x * x, axis=-1, keepdims=True)
    return x * lax.rsqrt(ms + EPS) * g


def _sigmoid(x):
    return 0.5 * jnp.tanh(0.5 * x) + 0.5


def _silu(x):
    h = 0.5 * x
    return h + h * jnp.tanh(h)


def _params(n_axes=1):
    return pltpu.CompilerParams(dimension_semantics=("arbitrary",) * n_axes,
                                vmem_limit_bytes=VMEM_LIMIT)


def _const_spec(shape):
    nd = len(shape)
    return pl.BlockSpec(shape, lambda i: (0,) * nd, pipeline_mode=pl.Buffered(1))


def _row_spec(tm, width, rev_n=None):
    if rev_n is None:
        return pl.BlockSpec((tm, width), lambda i: (i, 0))
    return pl.BlockSpec((tm, width), lambda i: (rev_n - 1 - i, 0))


def _halo_specs(tm, n_rows):
    hb = tm // HALO
    nh = n_rows // HALO
    prev = pl.BlockSpec((HALO, D_MODEL), lambda i: (jnp.maximum(i * hb - 1, 0), 0))
    main = pl.BlockSpec((tm, D_MODEL), lambda i: (i, 0))
    nxt = pl.BlockSpec((HALO, D_MODEL), lambda i: (jnp.minimum((i + 1) * hb, nh - 1), 0))
    return [prev, main, nxt]


def _fill_normed(xp_ref, xm_ref, xn_ref, g_ref, h_scr, tm):
    i = pl.program_id(0)
    n = pl.num_programs(0)
    g = g_ref[...]
    hp = _rms(xp_ref[...], g)
    hn = _rms(xn_ref[...], g)
    h_scr[0:HALO, :] = jnp.where(i > 0, hp, 0.0).astype(BF16)
    h_scr[HALO:HALO + tm, :] = _rms(xm_ref[...], g).astype(BF16)
    h_scr[HALO + tm:HALO + tm + HALO, :] = jnp.where(i < n - 1, hn, 0.0).astype(BF16)


def _proj_conv(h_ext, w_ref, u_scr, cw_ref, cb_ref, tm, c0, width):
    u = _dot(h_ext, w_ref[:, c0:c0 + width])
    base = HALO - CONV_LEFT
    outs = []
    for j in range(width // LANES):
        slab = c0 // LANES + j
        u_scr[slab] = u[:, j * LANES:(j + 1) * LANES]
        cs_ = slice(c0 + j * LANES, c0 + (j + 1) * LANES)
        acc = cb_ref[:, cs_] + cw_ref[0:1, cs_] * u_scr[slab, base:base + tm, :]
        for k in range(1, CONV_W):
            acc = acc + cw_ref[k:k + 1, cs_] * u_scr[slab, base + k:base + k + tm, :]
        outs.append(acc)
    return outs


def _even_in_body(xp_ref, xm_ref, xn_ref, g_ref, w_ref, cw_ref, cb_ref, dtb_ref, cos_ref, sa_ref, sb_ref,
                  z_ref, xc_ref, b_ref, c_ref, dt_ref, q_ref, kz_ref, vz_ref, sg_ref,
                  h_scr, u_scr, *, tm):
    cuts = [0]
    for width in EVEN_W_SIZES:
        cuts.append(cuts[-1] + width)
    wz_ref, wx_ref, wq_ref, wk_ref, wv_ref, wg_ref, wdt_ref = (
        w_ref.at[:, cuts[k]:cuts[k + 1]] for k in range(len(EVEN_W_SIZES)))
    _fill_normed(xp_ref, xm_ref, xn_ref, g_ref, h_scr, tm)
    h_ext = h_scr[...]
    hm = h_scr[HALO:HALO + tm, :]
    cstep = 256
    cos_t = cos_ref[...]
    sin_a = sa_ref[...]
    sin_b = sb_ref[...]
    low = lax.broadcasted_iota(jnp.int32, (tm, LANES), 1) < ATTN_HEAD_DIM

    def rope(t):
        return (t * cos_t + pltpu.roll(t, LANES - ROPE_DIM // 2, 1) * sin_a
                + pltpu.roll(t, ROPE_DIM // 2, 1) * sin_b)

    def do_xbc(c0):
        outs = _proj_conv(h_ext, wx_ref, u_scr, cw_ref, cb_ref, tm, c0, cstep)
        for j, acc in enumerate(outs):
            y = _silu(acc)
            c = c0 + j * LANES
            if c < SSD_INNER:
                xc_ref[:, c:c + LANES] = y
            elif c < SSD_INNER + 256:
                b_ref[:, c - SSD_INNER:c - SSD_INNER + LANES] = y
            else:
                c_ref[:, c - SSD_INNER - 256:c - SSD_INNER - 256 + LANES] = y

    def do_z(c0):
        z_ref[:, c0:c0 + cstep] = _silu(_dot(hm, wz_ref[:, c0:c0 + cstep]))

    def do_gate(c0):
        sg_ref[:, c0:c0 + cstep] = _silu(_dot(hm, wg_ref[:, c0:c0 + cstep]))

    def do_q(c0):
        qf = _dot(hm, wq_ref[:, c0:c0 + cstep]) * (ATTN_HEAD_DIM ** -0.5)
        for j in range(cstep // LANES):
            q_ref[:, c0 + j * LANES:c0 + (j + 1) * LANES] = rope(qf[:, j * LANES:(j + 1) * LANES]).astype(BF16)

    def do_kv(w_ref, dst, rotary):
        f = _dot(hm, w_ref[...])
        for j in range(2):
            a = f[:, j * LANES:(j + 1) * LANES]
            if rotary:
                a = rope(a)
            r = pltpu.roll(a, ATTN_HEAD_DIM, 1)
            c = 4 * j * LANES
            dst[:, c:c + LANES] = jnp.where(low, a, 0.0).astype(BF16)
            dst[:, c + LANES:c + 2 * LANES] = jnp.where(low, 0.0, r).astype(BF16)
            dst[:, c + 2 * LANES:c + 3 * LANES] = jnp.where(low, r, 0.0).astype(BF16)
            dst[:, c + 3 * LANES:c + 4 * LANES] = jnp.where(low, 0.0, a).astype(BF16)

    for n in range(4):
        do_xbc(n * cstep)
        do_z(n * cstep)
        do_q(n * cstep)
        do_gate(n * cstep)
        if n == 0:
            do_xbc(4 * cstep)
        elif n == 1:
            do_xbc(5 * cstep)
        elif n == 2:
            do_kv(wk_ref, kz_ref, True)
        else:
            do_kv(wv_ref, vz_ref, False)
    dt_ref[...] = jax.nn.softplus(_dot(hm, wdt_ref[...]) + dtb_ref[...])


def _even_in(x, g, w_packed, cw, cb, dtb, cos_t, sin_a, sin_b):
    n_rows = x.shape[0]
    tm = IN_TILE_EVEN
    rows = functools.partial(_row_spec, tm)
    in_specs = _halo_specs(tm, n_rows) + [
        _const_spec((1, D_MODEL)), _const_spec(w_packed.shape),
        _const_spec(cw.shape), _const_spec(cb.shape), _const_spec(dtb.shape),
        rows(LANES), rows(LANES), rows(LANES),
    ]
    out_shape = (
        jax.ShapeDtypeStruct((n_rows, SSD_INNER), F32),
        jax.ShapeDtypeStruct((n_rows, SSD_INNER), F32),
        jax.ShapeDtypeStruct((n_rows, 256), F32),
        jax.ShapeDtypeStruct((n_rows, 256), F32),
        jax.ShapeDtypeStruct((n_rows, LANES), F32),
        jax.ShapeDtypeStruct((n_rows, D_MODEL), BF16),
        jax.ShapeDtypeStruct((n_rows, D_MODEL), BF16),
        jax.ShapeDtypeStruct((n_rows, D_MODEL), BF16),
        jax.ShapeDtypeStruct((n_rows, D_MODEL), F32),
    )
    out_specs = (rows(SSD_INNER), rows(SSD_INNER), rows(256), rows(256), rows(LANES),
                 rows(D_MODEL), rows(D_MODEL), rows(D_MODEL), rows(D_MODEL))
    return pl.pallas_call(
        functools.partial(_even_in_body, tm=tm),
        grid=(n_rows // tm,),
        in_specs=in_specs, out_specs=out_specs, out_shape=out_shape,
        scratch_shapes=[pltpu.VMEM((tm + 2 * HALO, D_MODEL), BF16),
                        pltpu.VMEM((SSD_CONV_CH // LANES, tm + 2 * HALO, LANES), F32)],
        compiler_params=_params(), name="even_in",
    )(x, x, x, g, w_packed, cw, cb, dtb, cos_t, sin_a, sin_b)


def _ssd_body(*refs, reverse, combine):
    if combine:
        (xc_ref, b_ref, c_ref, dt_ref, alog_ref, e_ref, z_ref, yb_ref, dsk_ref, nw_ref,
         y_ref, state, cb_scr, yoff_scr, ytmp) = refs
    else:
        xc_ref, b_ref, c_ref, dt_ref, alog_ref, e_ref, y_ref, state, cb_scr, yoff_scr = refs
        z_ref = yb_ref = dsk_ref = nw_ref = ytmp = None

    @pl.when(pl.program_id(0) == 0)
    def _():
        state[...] = jnp.zeros_like(state)

    chunks = list(range(SSD_CHUNKS_PER_STEP))
    if reverse:
        chunks = chunks[::-1]
    for ci in chunks:
        rs = pl.ds(ci * SSD_CHUNK, SSD_CHUNK)
        y_view = y_ref.at[rs]
        _ssd_chunk(xc_ref.at[rs], b_ref.at[rs], c_ref.at[rs], dt_ref.at[rs], alog_ref, e_ref,
                   z_ref.at[rs] if combine else None, yb_ref.at[rs] if combine else None,
                   dsk_ref, nw_ref, y_view, state, cb_scr, yoff_scr,
                   ytmp if combine else y_view, reverse=reverse, combine=combine)


def _ssd_chunk(xc_ref, b_ref, c_ref, dt_ref, alog_ref, e_ref, z_ref, yb_ref, dsk_ref, nw_ref,
               y_ref, state, cb_scr, yoff_scr, ytmp, *, reverse, combine):
    t_len = SSD_CHUNK
    off = SSD_HEADS if reverse else 0

    row = lax.broadcasted_iota(jnp.int32, (t_len, t_len), 0)
    col = lax.broadcasted_iota(jnp.int32, (t_len, t_len), 1)
    mask = (col >= row) if reverse else (col <= row)
    tri = jnp.where(mask, 1.0, 0.0).astype(BF16)

    lane = lax.broadcasted_iota(jnp.int32, (1, LANES), 1)
    in_dir = (lane >= off) & (lane < off + SSD_HEADS)
    a_rate = jnp.where(in_dir, -LOG2E * jnp.exp(alog_ref[...]), 0.0)

    dt = dt_ref[...]
    cs = _dot3(tri, *_split3(dt * a_rate))
    log_dt = jnp.log2(dt)
    cs_t = (cs - log_dt).T
    last = 0 if reverse else t_len - 1
    cs_last = cs[last:last + 1, :]
    w_hi, w_mid, _ = _split3(jnp.exp2(cs_last - cs + log_dt))
    expand = e_ref[...]
    wdt_x = _dot(w_hi, expand) + _dot(w_mid, expand)

    b_f = b_ref[...]
    c_b = c_ref[...].astype(BF16)
    b_b = b_f.astype(BF16)
    b_t = b_f.T.astype(BF16)
    low = lax.broadcasted_iota(jnp.int32, (t_len, LANES), 1) < SSD_HEAD_DIM

    hb = t_len // 2
    key_blocks = ((0, 1), (1,)) if reverse else ((0,), (0, 1))
    diag_mask = mask[0:hb, 0:hb]
    low_hb = lax.broadcasted_iota(jnp.int32, (hb, LANES), 1) < SSD_HEAD_DIM

    gw = SSD_INNER // SSD_GROUPS
    for g in range(SSD_GROUPS):
        c_g = c_b[:, g * SSD_STATE:(g + 1) * SSD_STATE]
        b_g = b_b[:, g * SSD_STATE:(g + 1) * SSD_STATE]
        cb_scr[...] = lax.dot_general(c_g, b_g, (((1,), (1,)), ((), ())), preferred_element_type=F32)
        s_g = state[:, g * gw:(g + 1) * gw]
        yoff_scr[...] = _dot(c_g, s_g.astype(BF16))
        edge_decay = []
        for tt in range(gw // LANES):
            t = g * (gw // LANES) + tt
            ts = slice(t * LANES, (t + 1) * LANES)
            xt = xc_ref[:, ts]
            x_halves = (jnp.where(low, xt, 0.0).astype(BF16), jnp.where(low, 0.0, xt).astype(BF16))
            for rb in range(2):
                rs = slice(rb * hb, (rb + 1) * hb)
                cols = [jnp.broadcast_to(cs[rs, off + 2 * t + half:off + 2 * t + half + 1], (hb, LANES))
                        for half in range(2)]
                e_tile = jnp.where(low_hb, jnp.exp2(cols[0]), jnp.exp2(cols[1]))
                if rb * hb <= last < (rb + 1) * hb:
                    edge_decay.append(e_tile[last - rb * hb:last - rb * hb + 1, :])
                acc = yoff_scr[rs, tt * LANES:(tt + 1) * LANES] * e_tile
                for half in range(2):
                    ln = off + 2 * t + half
                    pieces = []
                    for kb in key_blocks[rb]:
                        ks = slice(kb * hb, (kb + 1) * hb)
                        seg = cols[half] - cs_t[ln:ln + 1, ks]
                        if kb == rb:
                            seg = jnp.where(diag_mask, seg, NEG)
                        pieces.append((cb_scr[rs, ks] * jnp.exp2(seg)).astype(BF16))
                    m = pieces[0] if len(pieces) == 1 else jnp.concatenate(pieces, axis=1)
                    k0, k1 = key_blocks[rb][0] * hb, (key_blocks[rb][-1] + 1) * hb
                    acc = acc + _dot(m, x_halves[half][k0:k1, :])
                ytmp[rs, ts] = acc.astype(ytmp.dtype)
        gs = slice(g * gw, (g + 1) * gw)
        xw_b = (xc_ref[:, gs] * wdt_x[:, gs]).astype(BF16)
        state[:, gs] = (jnp.concatenate(edge_decay, axis=1) * s_g
                        + _dot(b_t[g * SSD_STATE:(g + 1) * SSD_STATE, :], xw_b))

    if combine:
        y = ytmp[...] + yb_ref[...] + xc_ref[...] * dsk_ref[...]
        y = y * z_ref[...]
        for g in range(SSD_GROUPS):
            yg = y[:, g * gw:(g + 1) * gw]
            ms = jnp.mean(yg * yg, axis=-1, keepdims=True)
            y_ref[:, g * gw:(g + 1) * gw] = (
                yg * lax.rsqrt(ms + EPS) * nw_ref[:, g * gw:(g + 1) * gw]).astype(BF16)


def _ssd(xc, bm, cm, dt, alog_row, expand, reverse, combine_args=None):
    n_rows = xc.shape[0]
    t_len = SSD_CHUNK
    tile = SSD_CHUNKS_PER_STEP * t_len
    nc = n_rows // tile
    rows = functools.partial(_row_spec, tile, rev_n=nc if reverse else None)
    combine = combine_args is not None
    in_specs = [rows(SSD_INNER), rows(256), rows(256), rows(LANES),
                _const_spec((1, LANES)), _const_spec((LANES, SSD_INNER))]
    args = [xc, bm, cm, dt, alog_row, expand]
    scratch = [pltpu.VMEM((SSD_STATE, SSD_INNER), F32),
               pltpu.VMEM((t_len, t_len), F32),
               pltpu.VMEM((t_len, SSD_INNER // SSD_GROUPS), F32)]
    if combine:
        z, y_bwd, dskip_x, norm_w = combine_args
        in_specs += [rows(SSD_INNER), rows(SSD_INNER),
                     _const_spec((1, SSD_INNER)), _const_spec((1, SSD_INNER))]
        args += [z, y_bwd, dskip_x, norm_w]
        scratch.append(pltpu.VMEM((t_len, SSD_INNER), F32))
    out_dtype = BF16 if combine else F32
    return pl.pallas_call(
        functools.partial(_ssd_body, reverse=reverse, combine=combine),
        grid=(nc,),
        in_specs=in_specs, out_specs=rows(SSD_INNER),
        out_shape=jax.ShapeDtypeStruct((n_rows, SSD_INNER), out_dtype),
        scratch_shapes=scratch,
        compiler_params=_params(), name="ssd_fwd" if combine else "ssd_bwd",
    )(*args)


def _attn_body(sink_ref, q_ref, kp_ref, km_ref, kn_ref, vp_ref, vm_ref, vn_ref, sg_ref, ys_ref,
               x_ref, w_ref, o_ref, k_scr, v_scr, y_scr, yprev_scr, s_scr, *, tq):
    i = pl.program_id(0)
    n = pl.num_programs(0) - 1
    tile = jnp.minimum(i, n - 1)
    blk = ATTN_BLOCK
    nq = tq // blk
    wc = D_MODEL // nq

    @pl.when(i == 0)
    def _():
        y_scr[...] = jnp.zeros_like(y_scr)

    yprev_scr[...] = y_scr[...]
    for scr, p_ref, m_ref, n_ref in ((k_scr, kp_ref, km_ref, kn_ref), (v_scr, vp_ref, vm_ref, vn_ref)):
        scr[0:blk, :] = p_ref[...]
        scr[blk:blk + tq, :] = m_ref[...]
        scr[blk + tq:blk + tq + blk, :] = n_ref[...]

    row2 = lax.broadcasted_iota(jnp.int32, (2 * blk, blk), 0)
    qi = row2 & (blk - 1)
    kj = lax.broadcasted_iota(jnp.int32, (2 * blk, blk), 1)
    first_tile = row2[:, 0:1] < blk
    n_kv = ATTN_KV_HEADS

    for b in range(nq):
        r0 = b * blk
        gb = tile * nq + b
        bias_l = jnp.where((kj >= qi) & (gb > 0), 0.0, NEG)
        bias_r = jnp.where((kj <= qi) & (gb < n * nq - 1), 0.0, NEG)
        for hk in range(n_kv):
            q2 = jnp.concatenate(
                [q_ref[pl.ds(r0, blk), (2 * hk) * LANES:(2 * hk + 1) * LANES],
                 q_ref[pl.ds(r0, blk), (2 * hk + 1) * LANES:(2 * hk + 2) * LANES]], axis=0)
            kz = jnp.concatenate(
                [k_scr[pl.ds(r0, 3 * blk), (2 * hk + half) * LANES:(2 * hk + half + 1) * LANES]
                 for half in range(2)], axis=0)
            s_scr[hk] = lax.dot_general(q2, kz, (((1,), (1,)), ((), ())), preferred_element_type=F32)
        oc = slice(b * wc, (b + 1) * wc)
        o_ref[:, oc] = (x_ref[:, oc] + _dot(ys_ref[...], w_ref[0:SSD_INNER, oc])
                        + _dot(yprev_scr[...], w_ref[SSD_INNER:SSD_INNER + D_MODEL, oc]))
        for hk in range(n_kv):
            out = None
            for half in range(2):
                c = (2 * hk + half) * LANES
                k0 = half * 3 * blk
                s_l = s_scr[hk, :, k0:k0 + blk] + bias_l
                s_m = s_scr[hk, :, k0 + blk:k0 + 2 * blk]
                s_r = s_scr[hk, :, k0 + 2 * blk:k0 + 3 * blk] + bias_r
                sink = jnp.where(first_tile, sink_ref[4 * hk + half], sink_ref[4 * hk + 2 + half])
                m = jnp.max(jnp.maximum(jnp.maximum(s_l, s_m), s_r), axis=-1, keepdims=True)
                m = jnp.maximum(m, sink)
                p_l = jnp.exp(s_l - m)
                p_m = jnp.exp(s_m - m)
                p_r = jnp.exp(s_r - m)
                den = jnp.sum(p_l + p_m + p_r, axis=-1, keepdims=True) + jnp.exp(sink - m)
                p = jnp.concatenate([p_l, p_m, p_r], axis=1).astype(BF16)
                o = _dot(p, v_scr[pl.ds(r0, 3 * blk), c:c + LANES]) * (1.0 / den)
                out = o if out is None else out + o
            for k in range(2):
                t = 2 * hk + k
                y_scr[pl.ds(r0, blk), t * LANES:(t + 1) * LANES] = (
                    out[k * blk:(k + 1) * blk, :]
                    * sg_ref[pl.ds(r0, blk), t * LANES:(t + 1) * LANES]).astype(BF16)


def _attn_out(sink, q, kz, vz, sg, y_ssd, x, w_out):
    n_rows = x.shape[0]
    tq = ATTN_TILE
    blk = ATTN_BLOCK
    per = tq // blk
    nb = n_rows // blk
    n = n_rows // tq

    def cur(i):
        return jnp.minimum(i, n - 1)

    def lag(i):
        return jnp.maximum(i - 1, 0)

    rows_cur = pl.BlockSpec((tq, D_MODEL), lambda i: (cur(i), 0))
    rows_lag = pl.BlockSpec((tq, D_MODEL), lambda i: (lag(i), 0))
    prev = pl.BlockSpec((blk, D_MODEL), lambda i: (jnp.maximum(cur(i) * per - 1, 0), 0))
    nxt = pl.BlockSpec((blk, D_MODEL), lambda i: (jnp.minimum((cur(i) + 1) * per, nb - 1), 0))
    in_specs = [pl.BlockSpec(memory_space=pltpu.SMEM),
                rows_cur, prev, rows_cur, nxt, prev, rows_cur, nxt,
                rows_cur, rows_lag, rows_lag,
                _const_spec(w_out.shape)]
    return pl.pallas_call(
        functools.partial(_attn_body, tq=tq),
        grid=(n + 1,),
        in_specs=in_specs, out_specs=rows_lag,
        out_shape=jax.ShapeDtypeStruct((n_rows, D_MODEL), F32),
        scratch_shapes=[pltpu.VMEM((tq + 2 * blk, D_MODEL), BF16),
                        pltpu.VMEM((tq + 2 * blk, D_MODEL), BF16),
                        pltpu.VMEM((tq, D_MODEL), BF16),
                        pltpu.VMEM((tq, D_MODEL), BF16),
                        pltpu.VMEM((ATTN_KV_HEADS, 2 * blk, 6 * blk), F32)],
        compiler_params=_params(), name="attn_out",
    )(sink, q, kz, kz, kz, vz, vz, vz, sg, y_ssd, x, w_out)


def _odd_in_body(xp_ref, xm_ref, xn_ref, g_ref, w_ref, cw_ref, cb_ref,
                 xc_ref, sg_ref, h_scr, u_scr, *, tm):
    wx_ref = w_ref.at[:, 0:LRU_WIDTH]
    wg_ref = w_ref.at[:, LRU_WIDTH:2 * LRU_WIDTH]
    _fill_normed(xp_ref, xm_ref, xn_ref, g_ref, h_scr, tm)
    h_ext = h_scr[...]
    hm = h_scr[HALO:HALO + tm, :]
    cstep = 256
    for c0 in range(0, LRU_WIDTH, cstep):
        outs = _proj_conv(h_ext, wx_ref, u_scr, cw_ref, cb_ref, tm, c0, cstep)
        for j, acc in enumerate(outs):
            xc_ref[:, c0 + j * LANES:c0 + (j + 1) * LANES] = acc
        sg_ref[:, c0:c0 + cstep] = _silu(_dot(hm, wg_ref[:, c0:c0 + cstep]))


def _odd_in(x, g, w, cw, cb):
    n_rows = x.shape[0]
    tm = IN_TILE_ODD
    rows = functools.partial(_row_spec, tm)
    in_specs = _halo_specs(tm, n_rows) + [
        _const_spec((1, D_MODEL)), _const_spec(w.shape),
        _const_spec(cw.shape), _const_spec(cb.shape)]
    return pl.pallas_call(
        functools.partial(_odd_in_body, tm=tm),
        grid=(n_rows // tm,),
        in_specs=in_specs, out_specs=(rows(LRU_WIDTH), rows(LRU_WIDTH)),
        out_shape=(jax.ShapeDtypeStruct((n_rows, LRU_WIDTH), F32),
                   jax.ShapeDtypeStruct((n_rows, LRU_WIDTH), F32)),
        scratch_shapes=[pltpu.VMEM((tm + 2 * HALO, D_MODEL), BF16),
                        pltpu.VMEM((LRU_WIDTH // LANES, tm + 2 * HALO, LANES), F32)],
        compiler_params=_params(), name="odd_in",
    )(x, x, x, g, w, cw, cb)


def _lru_body(*refs, reverse, combine, final):
    if combine:
        if final:
            (xc_ref, w_ref, ba_ref, bx_ref, lam_ref, hb_ref, sg_ref, x_ref, wo_ref, fw_ref,
             o_ref, a_scr, u_scr, carry, n_scr, y_scr) = refs
        else:
            (xc_ref, w_ref, ba_ref, bx_ref, lam_ref, hb_ref, sg_ref, x_ref, wo_ref,
             o_ref, a_scr, u_scr, carry, n_scr, y_scr) = refs
    else:
        xc_ref, w_ref, ba_ref, bx_ref, lam_ref, o_ref, a_scr, u_scr, carry = refs
    tm, sub, pitch = LRU_TILE, LRU_SUB, LRU_PITCH
    vregs_per_sub = sub // SUBLANES
    grp_per_blk = LRU_BLOCK // LANES
    half_w = LRU_HALF_GROUPS * LANES

    @pl.when(pl.program_id(0) == 0)
    def _():
        carry[...] = jnp.zeros_like(carry)

    half_rate = (-0.5 * LRU_C) * jax.nn.softplus(-lam_ref[...])
    steps = list(range(sub))
    order = list(range(SUBLANES))
    if reverse:
        steps = steps[::-1]
        order = order[::-1]

    for blk in range(LRU_BLOCKS):
        c0 = blk * LRU_BLOCK
        xb = xc_ref[:, c0:c0 + LRU_BLOCK]
        ri = _dot(xb.astype(BF16), w_ref[blk])
        hr = half_rate[:, c0:c0 + LRU_BLOCK]
        log_a = hr + hr * jnp.tanh(0.5 * (ri[:, :LRU_BLOCK] + ba_ref[:, c0:c0 + LRU_BLOCK]))
        hx = 0.5 * xb
        gx = hx + hx * jnp.tanh(0.5 * (ri[:, LRU_BLOCK:] + bx_ref[:, c0:c0 + LRU_BLOCK]))
        a = jnp.exp(log_a)
        th = jnp.tanh(log_a)
        num = -2.0 * th
        root = jnp.where(num > 0.0, num * lax.rsqrt(num), 0.0) * lax.rsqrt(1.0 - th)
        u = root * gx
        for half in range(grp_per_blk):
            g = blk * grp_per_blk + half
            ls = slice(half * LANES, (half + 1) * LANES)
            for k in range(tm // SUBLANES):
                s = k // vregs_per_sub
                jj = (k % vregs_per_sub) * SUBLANES
                dst = pl.ds(jj * SUBLANES + s, SUBLANES, stride=SUBLANES)
                a_scr[g, dst, :] = a[k * SUBLANES:(k + 1) * SUBLANES, ls]
                u_scr[g, dst, :] = u[k * SUBLANES:(k + 1) * SUBLANES, ls]

    def rows_t(j):
        return slice(j * SUBLANES, (j + 1) * SUBLANES)

    for hf in range(LRU_WIDTH // half_w):
        gs = slice(hf * LRU_HALF_GROUPS, (hf + 1) * LRU_HALF_GROUPS)
        h = u_scr[gs, rows_t(steps[0]), :]
        p = a_scr[gs, rows_t(steps[0]), :]
        for j in steps[1:]:
            aj = a_scr[gs, rows_t(j), :]
            h = aj * h + u_scr[gs, rows_t(j), :]
            p = aj * p
            u_scr[gs, rows_t(j), :] = h
            a_scr[gs, rows_t(j), :] = p

        c = carry[gs]
        carry_in = [None] * SUBLANES
        for s in order:
            carry_in[s] = c
            c = p[:, s:s + 1, :] * c + h[:, s:s + 1, :]
        carry[gs] = c
        cmat = jnp.concatenate(carry_in, axis=1)

        for j in range(sub):
            hj = u_scr[gs, rows_t(j), :] + a_scr[gs, rows_t(j), :] * cmat
            for g in range(LRU_HALF_GROUPS):
                cs_ = slice(hf * half_w + g * LANES, hf * half_w + (g + 1) * LANES)
                if combine:
                    n_scr[g, pl.ds(j, SUBLANES, stride=pitch), :] = hj[g] + hb_ref[rows_t(j), cs_]
                else:
                    o_ref[rows_t(j), cs_] = hj[g]

        if combine:
            for s in range(SUBLANES):
                rs = slice(s * sub, (s + 1) * sub)
                for g in range(LRU_HALF_GROUPS):
                    cs_ = slice(hf * half_w + g * LANES, hf * half_w + (g + 1) * LANES)
                    y_scr[rs, cs_] = (n_scr[g, s * pitch:s * pitch + sub, :] * sg_ref[rs, cs_]).astype(BF16)

    if combine:
        out = x_ref[...] + _dot(y_scr[...], wo_ref[...])
        if final:
            out = _rms(out, fw_ref[...])
        o_ref[...] = out


def _lru(xc, w_gates, b_a, b_x, lam, reverse, combine_args=None, final_w=None):
    n_rows = xc.shape[0]
    tm = LRU_TILE
    n = n_rows // tm
    combine = combine_args is not None
    final = final_w is not None
    rows = functools.partial(_row_spec, tm, rev_n=n if reverse else None)
    in_specs = [rows(LRU_WIDTH), _const_spec(w_gates.shape), _const_spec((1, LRU_WIDTH)),
                _const_spec((1, LRU_WIDTH)), _const_spec((1, LRU_WIDTH))]
    args = [xc, w_gates, b_a, b_x, lam]
    n_grp = LRU_WIDTH // LANES
    scratch = [pltpu.VMEM((n_grp, tm, LANES), F32),
               pltpu.VMEM((n_grp, tm, LANES), F32),
               pltpu.VMEM((n_grp, 1, LANES), F32)]
    if combine:
        h_bwd, sg, x, w_out = combine_args
        in_specs += [rows(LRU_WIDTH), rows(LRU_WIDTH), rows(D_MODEL), _const_spec(w_out.shape)]
        args += [h_bwd, sg, x, w_out]
        if final:
            in_specs.append(_const_spec((1, D_MODEL)))
            args.append(final_w)
        scratch.append(pltpu.VMEM((LRU_HALF_GROUPS, SUBLANES * LRU_PITCH, LANES), F32))
        scratch.append(pltpu.VMEM((tm, LRU_WIDTH), BF16))
        out_w = D_MODEL
    else:
        out_w = LRU_WIDTH
    return pl.pallas_call(
        functools.partial(_lru_body, reverse=reverse, combine=combine, final=final),
        grid=(n,),
        in_specs=in_specs, out_specs=rows(out_w),
        out_shape=jax.ShapeDtypeStruct((n_rows, out_w), F32),
        scratch_shapes=scratch,
        compiler_params=_params(), name="lru_fwd" if combine else "lru_bwd",
    )(*args)


def _rope_tables(n_rows):
    inv = ROPE_THETA ** (-jnp.arange(0, ROPE_DIM, 2, dtype=F32) / ROPE_DIM)
    ang = jnp.arange(n_rows, dtype=F32)[:, None] * inv[None, :]
    cos, sin = jnp.cos(ang), jnp.sin(ang)
    half = ROPE_DIM // 2
    ones = jnp.ones((n_rows, ATTN_HEAD_DIM - ROPE_DIM), F32)
    zeros = jnp.zeros((n_rows, ATTN_HEAD_DIM - ROPE_DIM), F32)
    zero_h = jnp.zeros((n_rows, half), F32)
    reps = LANES // ATTN_HEAD_DIM
    cos_t = jnp.concatenate([cos, cos, ones] * reps, axis=1)
    sin_a = jnp.concatenate([-sin, zero_h, zeros] * reps, axis=1)
    sin_b = jnp.concatenate([zero_h, sin, zeros] * reps, axis=1)
    return cos_t, sin_a, sin_b


def _pad_lanes(v):
    return jnp.pad(v.astype(F32), (0, LANES - v.shape[0]))[None, :]


def _even_layer(x, g_norm, w_in, conv_w, conv_b, dt_bias, a_log, d_skip, ssd_norm_w, sink, w_out, rope):
    n_dt = 2 * SSD_HEADS
    dt0 = SSD_INNER + SSD_CONV_CH
    w_packed = jnp.concatenate(
        [w_in[:, :dt0], w_in[:, dt0 + n_dt:], w_in[:, dt0:dt0 + n_dt],
         jnp.zeros((D_MODEL, LANES - n_dt), w_in.dtype)], axis=1).astype(BF16)
    dtb = _pad_lanes(dt_bias.reshape(-1))
    z, xc, bm, cm, dt, q, kz, vz, sg = _even_in(
        x, g_norm[None, :], w_packed, conv_w, conv_b[None, :], dtb, *rope)

    lane = jnp.arange(LANES)[:, None]
    head = (jnp.arange(SSD_INNER) // SSD_HEAD_DIM)[None, :]
    alog_row = _pad_lanes(a_log.reshape(-1))
    y_bwd = _ssd(xc, bm, cm, dt, alog_row, (lane == head + SSD_HEADS).astype(BF16), reverse=True)
    dskip_x = jnp.repeat(d_skip.astype(F32), SSD_HEAD_DIM)[None, :]
    y_ssd = _ssd(xc, bm, cm, dt, alog_row, (lane == head).astype(BF16), reverse=False,
                 combine_args=(z, y_bwd, dskip_x, ssd_norm_w[None, :]))

    return _attn_out(sink.astype(F32), q, kz, vz, sg, y_ssd, x, w_out.astype(BF16))


def _odd_layer(x, g_norm, w_in, conv_w, conv_b, w_a, b_a, w_x, b_x, lam, w_out, final_w):
    xc, sg = _odd_in(x, g_norm[None, :], w_in.astype(BF16), conv_w, conv_b[None, :])
    w_gates = jnp.concatenate([w_a, w_x], axis=-1).astype(BF16)
    h_bwd = _lru(xc, w_gates[1], b_a[1][None, :], b_x[1][None, :], lam[1][None, :], reverse=True)
    return _lru(xc, w_gates[0], b_a[0][None, :], b_x[0][None, :], lam[0][None, :], reverse=False,
                combine_args=(h_bwd, sg, x, w_out.astype(BF16)),
                final_w=None if final_w is None else final_w[None, :])


def kernel(x, norm_w, final_norm_w, ev_w_in, ev_conv_w, ev_conv_b, ev_dt_bias, ev_a_log, ev_d_skip, ev_ssd_norm_w, ev_sink, ev_w_out, od_w_in, od_conv_w, od_conv_b, od_w_a, od_b_a, od_w_x, od_b_x, od_lambda, od_w_out):
    batch, n_rows, _ = x.shape
    depth = norm_w.shape[0]
    assert batch == 1 and depth % 2 == 0
    rope = _rope_tables(n_rows)
    h = x[0]
    for layer in range(depth):
        j = layer // 2
        if layer % 2 == 0:
            h = _even_layer(h, norm_w[layer], ev_w_in[j], ev_conv_w[j], ev_conv_b[j], ev_dt_bias[j],
                            ev_a_log[j], ev_d_skip[j], ev_ssd_norm_w[j], ev_sink[j], ev_w_out[j], rope)
        else:
            h = _odd_layer(h, norm_w[layer], od_w_in[j], od_conv_w[j], od_conv_b[j], od_w_a[j],
                           od_b_a[j], od_w_x[j], od_b_x[j], od_lambda[j], od_w_out[j],
                           final_norm_w if layer == depth - 1 else None)
    return h[None]
```

```python
import functools

import jax
import jax.numpy as jnp
from jax import lax
from jax.experimental import pallas as pl
from jax.experimental.pallas import tpu as pltpu

F32 = jnp.float32
BF16 = jnp.bfloat16

D_MODEL = 1024
EPS = 1e-6
LANES = 128
SUBLANES = 8
HALO = 16
CONV_W = 4
CONV_LEFT = 2

SSD_HEADS = 16
SSD_HEAD_DIM = 64
SSD_STATE = 128
SSD_GROUPS = 2
SSD_CHUNK = 256
SSD_CHUNKS_PER_STEP = 2
SSD_INNER = 1024
SSD_CONV_CH = 1536

EVEN_W_SIZES = (1024, 1536, 1024, 256, 256, 1024, 128)

ATTN_HEADS = 16
ATTN_KV_HEADS = 4
ATTN_HEAD_DIM = 64
ATTN_BLOCK = 128
ROPE_THETA = 500000.0
ROPE_DIM = 16

LRU_WIDTH = 2048
LRU_BLOCK = 256
LRU_BLOCKS = 8
LRU_C = 8.0

LOG2E = 1.4426950408889634
NEG = -1e30
VMEM_LIMIT = 56 * 1024 * 1024

IN_TILE_EVEN = 512
IN_TILE_ODD = 512
ATTN_TILE = 512
LRU_TILE = 512
LRU_SUB = LRU_TILE // SUBLANES
LRU_PITCH = LRU_SUB + SUBLANES
LRU_HALF_GROUPS = 8


def _dot(a, b):
    return jnp.dot(a, b, preferred_element_type=F32)


def _split3(v):
    hi = v.astype(BF16)
    r1 = v - hi.astype(F32)
    mid = r1.astype(BF16)
    lo = (r1 - mid.astype(F32)).astype(BF16)
    return hi, mid, lo


def _dot3(*ops, rhs_split=True):
    if rhs_split:
        m, parts = ops[0], ops[1:]
        return _dot(m, parts[0]) + _dot(m, parts[1]) + _dot(m, parts[2])
    parts, m = ops[:3], ops[3]
    return _dot(parts[0], m) + _dot(parts[1], m) + _dot(parts[2], m)


def _rms(x, g):
    ms = jnp.mean(x * x, axis=-1, keepdims=True)
    return x * lax.rsqrt(ms + EPS) * g


def _sigmoid(x):
    return 0.5 * jnp.tanh(0.5 * x) + 0.5


def _silu(x):
    h = 0.5 * x
    return h + h * jnp.tanh(h)


def _params(n_axes=1):
    return pltpu.CompilerParams(dimension_semantics=("arbitrary",) * n_axes,
                                vmem_limit_bytes=VMEM_LIMIT)


def _const_spec(shape):
    nd = len(shape)
    return pl.BlockSpec(shape, lambda i: (0,) * nd, pipeline_mode=pl.Buffered(1))


def _row_spec(tm, width, rev_n=None):
    if rev_n is None:
        return pl.BlockSpec((tm, width), lambda i: (i, 0))
    return pl.BlockSpec((tm, width), lambda i: (rev_n - 1 - i, 0))


def _halo_specs(tm, n_rows):
    hb = tm // HALO
    nh = n_rows // HALO
    prev = pl.BlockSpec((HALO, D_MODEL), lambda i: (jnp.maximum(i * hb - 1, 0), 0))
    main = pl.BlockSpec((tm, D_MODEL), lambda i: (i, 0))
    nxt = pl.BlockSpec((HALO, D_MODEL), lambda i: (jnp.minimum((i + 1) * hb, nh - 1), 0))
    return [prev, main, nxt]


def _fill_normed(xp_ref, xm_ref, xn_ref, g_ref, h_scr, tm):
    i = pl.program_id(0)
    n = pl.num_programs(0)
    g = g_ref[...]
    hp = _rms(xp_ref[...], g)
    hn = _rms(xn_ref[...], g)
    h_scr[0:HALO, :] = jnp.where(i > 0, hp, 0.0).astype(BF16)
    h_scr[HALO:HALO + tm, :] = _rms(xm_ref[...], g).astype(BF16)
    h_scr[HALO + tm:HALO + tm + HALO, :] = jnp.where(i < n - 1, hn, 0.0).astype(BF16)


def _proj_conv(h_ext, w_ref, u_scr, cw_ref, cb_ref, tm, c0, width):
    u = _dot(h_ext, w_ref[:, c0:c0 + width])
    base = HALO - CONV_LEFT
    outs = []
    for j in range(width // LANES):
        slab = c0 // LANES + j
        u_scr[slab] = u[:, j * LANES:(j + 1) * LANES]
        cs_ = slice(c0 + j * LANES, c0 + (j + 1) * LANES)
        acc = cb_ref[:, cs_] + cw_ref[0:1, cs_] * u_scr[slab, base:base + tm, :]
        for k in range(1, CONV_W):
            acc = acc + cw_ref[k:k + 1, cs_] * u_scr[slab, base + k:base + k + tm, :]
        outs.append(acc)
    return outs


def _even_in_body(xp_ref, xm_ref, xn_ref, g_ref, w_ref, cw_ref, cb_ref, dtb_ref, cos_ref, sa_ref, sb_ref,
                  z_ref, xc_ref, b_ref, c_ref, dt_ref, q_ref, kz_ref, vz_ref, sg_ref,
                  h_scr, u_scr, *, tm):
    cuts = [0]
    for width in EVEN_W_SIZES:
        cuts.append(cuts[-1] + width)
    wz_ref, wx_ref, wq_ref, wk_ref, wv_ref, wg_ref, wdt_ref = (
        w_ref.at[:, cuts[k]:cuts[k + 1]] for k in range(len(EVEN_W_SIZES)))
    _fill_normed(xp_ref, xm_ref, xn_ref, g_ref, h_scr, tm)
    h_ext = h_scr[...]
    hm = h_scr[HALO:HALO + tm, :]
    cstep = 256
    cos_t = cos_ref[...]
    sin_a = sa_ref[...]
    sin_b = sb_ref[...]
    low = lax.broadcasted_iota(jnp.int32, (tm, LANES), 1) < ATTN_HEAD_DIM

    def rope(t):
        return (t * cos_t + pltpu.roll(t, LANES - ROPE_DIM // 2, 1) * sin_a
                + pltpu.roll(t, ROPE_DIM // 2, 1) * sin_b)

    def do_xbc(c0):
        outs = _proj_conv(h_ext, wx_ref, u_scr, cw_ref, cb_ref, tm, c0, cstep)
        for j, acc in enumerate(outs):
            y = _silu(acc)
            c = c0 + j * LANES
            if c < SSD_INNER:
                xc_ref[:, c:c + LANES] = y
            elif c < SSD_INNER + 256:
                b_ref[:, c - SSD_INNER:c - SSD_INNER + LANES] = y
            else:
                c_ref[:, c - SSD_INNER - 256:c - SSD_INNER - 256 + LANES] = y

    def do_z(c0):
        z_ref[:, c0:c0 + cstep] = _silu(_dot(hm, wz_ref[:, c0:c0 + cstep]))

    def do_gate(c0):
        sg_ref[:, c0:c0 + cstep] = _silu(_dot(hm, wg_ref[:, c0:c0 + cstep]))

    def do_q(c0):
        qf = _dot(hm, wq_ref[:, c0:c0 + cstep]) * (ATTN_HEAD_DIM ** -0.5)
        for j in range(cstep // LANES):
            q_ref[:, c0 + j * LANES:c0 + (j + 1) * LANES] = rope(qf[:, j * LANES:(j + 1) * LANES]).astype(BF16)

    def do_kv(w_ref, dst, rotary):
        f = _dot(hm, w_ref[...])
        for j in range(2):
            a = f[:, j * LANES:(j + 1) * LANES]
            if rotary:
                a = rope(a)
            r = pltpu.roll(a, ATTN_HEAD_DIM, 1)
            c = 4 * j * LANES
            dst[:, c:c + LANES] = jnp.where(low, a, 0.0).astype(BF16)
            dst[:, c + LANES:c + 2 * LANES] = jnp.where(low, 0.0, r).astype(BF16)
            dst[:, c + 2 * LANES:c + 3 * LANES] = jnp.where(low, r, 0.0).astype(BF16)
            dst[:, c + 3 * LANES:c + 4 * LANES] = jnp.where(low, 0.0, a).astype(BF16)

    for n in range(4):
        do_xbc(n * cstep)
        do_z(n * cstep)
        do_q(n * cstep)
        do_gate(n * cstep)
        if n == 0:
            do_xbc(4 * cstep)
        elif n == 1:
            do_xbc(5 * cstep)
        elif n == 2:
            do_kv(wk_ref, kz_ref, True)
        else:
            do_kv(wv_ref, vz_ref, False)
    dt_ref[...] = jax.nn.softplus(_dot(hm, wdt_ref[...]) + dtb_ref[...])


def _even_in(x, g, w_packed, cw, cb, dtb, cos_t, sin_a, sin_b):
    n_rows = x.shape[0]
    tm = IN_TILE_EVEN
    rows = functools.partial(_row_spec, tm)
    in_specs = _halo_specs(tm, n_rows) + [
        _const_spec((1, D_MODEL)), _const_spec(w_packed.shape),
        _const_spec(cw.shape), _const_spec(cb.shape), _const_spec(dtb.shape),
        rows(LANES), rows(LANES), rows(LANES),
    ]
    out_shape = (
        jax.ShapeDtypeStruct((n_rows, SSD_INNER), F32),
        jax.ShapeDtypeStruct((n_rows, SSD_INNER), F32),
        jax.ShapeDtypeStruct((n_rows, 256), F32),
        jax.ShapeDtypeStruct((n_rows, 256), F32),
        jax.ShapeDtypeStruct((n_rows, LANES), F32),
        jax.ShapeDtypeStruct((n_rows, D_MODEL), BF16),
        jax.ShapeDtypeStruct((n_rows, D_MODEL), BF16),
        jax.ShapeDtypeStruct((n_rows, D_MODEL), BF16),
        jax.ShapeDtypeStruct((n_rows, D_MODEL), F32),
    )
    out_specs = (rows(SSD_INNER), rows(SSD_INNER), rows(256), rows(256), rows(LANES),
                 rows(D_MODEL), rows(D_MODEL), rows(D_MODEL), rows(D_MODEL))
    return pl.pallas_call(
        functools.partial(_even_in_body, tm=tm),
        grid=(n_rows // tm,),
        in_specs=in_specs, out_specs=out_specs, out_shape=out_shape,
        scratch_shapes=[pltpu.VMEM((tm + 2 * HALO, D_MODEL), BF16),
                        pltpu.VMEM((SSD_CONV_CH // LANES, tm + 2 * HALO, LANES), F32)],
        compiler_params=_params(), name="even_in",
    )(x, x, x, g, w_packed, cw, cb, dtb, cos_t, sin_a, sin_b)


def _ssd_body(*refs, reverse, combine):
    if combine:
        (xc_ref, b_ref, c_ref, dt_ref, alog_ref, e_ref, z_ref, yb_ref, dsk_ref, nw_ref,
         y_ref, state, cb_scr, yoff_scr, ytmp) = refs
    else:
        xc_ref, b_ref, c_ref, dt_ref, alog_ref, e_ref, y_ref, state, cb_scr, yoff_scr = refs
        z_ref = yb_ref = dsk_ref = nw_ref = ytmp = None

    @pl.when(pl.program_id(0) == 0)
    def _():
        state[...] = jnp.zeros_like(state)

    chunks = list(range(SSD_CHUNKS_PER_STEP))
    if reverse:
        chunks = chunks[::-1]
    for ci in chunks:
        rs = pl.ds(ci * SSD_CHUNK, SSD_CHUNK)
        y_view = y_ref.at[rs]
        _ssd_chunk(xc_ref.at[rs], b_ref.at[rs], c_ref.at[rs], dt_ref.at[rs], alog_ref, e_ref,
                   z_ref.at[rs] if combine else None, yb_ref.at[rs] if combine else None,
                   dsk_ref, nw_ref, y_view, state, cb_scr, yoff_scr,
                   ytmp if combine else y_view, reverse=reverse, combine=combine)


def _ssd_chunk(xc_ref, b_ref, c_ref, dt_ref, alog_ref, e_ref, z_ref, yb_ref, dsk_ref, nw_ref,
               y_ref, state, cb_scr, yoff_scr, ytmp, *, reverse, combine):
    t_len = SSD_CHUNK
    off = SSD_HEADS if reverse else 0

    row = lax.broadcasted_iota(jnp.int32, (t_len, t_len), 0)
    col = lax.broadcasted_iota(jnp.int32, (t_len, t_len), 1)
    mask = (col >= row) if reverse else (col <= row)
    tri = jnp.where(mask, 1.0, 0.0).astype(BF16)

    lane = lax.broadcasted_iota(jnp.int32, (1, LANES), 1)
    in_dir = (lane >= off) & (lane < off + SSD_HEADS)
    a_rate = jnp.where(in_dir, -LOG2E * jnp.exp(alog_ref[...]), 0.0)

    dt = dt_ref[...]
    cs = _dot3(tri, *_split3(dt * a_rate))
    log_dt = jnp.log2(dt)
    cs_t = (cs - log_dt).T
    last = 0 if reverse else t_len - 1
    cs_last = cs[last:last + 1, :]
    w_hi, w_mid, _ = _split3(jnp.exp2(cs_last - cs + log_dt))
    expand = e_ref[...]
    wdt_x = _dot(w_hi, expand) + _dot(w_mid, expand)

    b_f = b_ref[...]
    c_b = c_ref[...].astype(BF16)
    b_b = b_f.astype(BF16)
    b_t = b_f.T.astype(BF16)
    low = lax.broadcasted_iota(jnp.int32, (t_len, LANES), 1) < SSD_HEAD_DIM

    hb = t_len // 2
    key_blocks = ((0, 1), (1,)) if reverse else ((0,), (0, 1))
    diag_mask = mask[0:hb, 0:hb]
    low_hb = lax.broadcasted_iota(jnp.int32, (hb, LANES), 1) < SSD_HEAD_DIM

    gw = SSD_INNER // SSD_GROUPS
    for g in range(SSD_GROUPS):
        c_g = c_b[:, g * SSD_STATE:(g + 1) * SSD_STATE]
        b_g = b_b[:, g * SSD_STATE:(g + 1) * SSD_STATE]
        cb_scr[...] = lax.dot_general(c_g, b_g, (((1,), (1,)), ((), ())), preferred_element_type=F32)
        s_g = state[:, g * gw:(g + 1) * gw]
        yoff_scr[...] = _dot(c_g, s_g.astype(BF16))
        edge_decay = []
        for tt in range(gw // LANES):
            t = g * (gw // LANES) + tt
            ts = slice(t * LANES, (t + 1) * LANES)
            xt = xc_ref[:, ts]
            x_halves = (jnp.where(low, xt, 0.0).astype(BF16), jnp.where(low, 0.0, xt).astype(BF16))
            for rb in range(2):
                rs = slice(rb * hb, (rb + 1) * hb)
                cols = [jnp.broadcast_to(cs[rs, off + 2 * t + half:off + 2 * t + half + 1], (hb, LANES))
                        for half in range(2)]
                e_tile = jnp.where(low_hb, jnp.exp2(cols[0]), jnp.exp2(cols[1]))
                if rb * hb <= last < (rb + 1) * hb:
                    edge_decay.append(e_tile[last - rb * hb:last - rb * hb + 1, :])
                acc = yoff_scr[rs, tt * LANES:(tt + 1) * LANES] * e_tile
                for half in range(2):
                    ln = off + 2 * t + half
                    pieces = []
                    for kb in key_blocks[rb]:
                        ks = slice(kb * hb, (kb + 1) * hb)
                        seg = cols[half] - cs_t[ln:ln + 1, ks]
                        if kb == rb:
                            seg = jnp.where(diag_mask, seg, NEG)
                        pieces.append((cb_scr[rs, ks] * jnp.exp2(seg)).astype(BF16))
                    m = pieces[0] if len(pieces) == 1 else jnp.concatenate(pieces, axis=1)
                    k0, k1 = key_blocks[rb][0] * hb, (key_blocks[rb][-1] + 1) * hb
                    acc = acc + _dot(m, x_halves[half][k0:k1, :])
                ytmp[rs, ts] = acc.astype(ytmp.dtype)
        gs = slice(g * gw, (g + 1) * gw)
        xw_b = (xc_ref[:, gs] * wdt_x[:, gs]).astype(BF16)
        state[:, gs] = (jnp.concatenate(edge_decay, axis=1) * s_g
                        + _dot(b_t[g * SSD_STATE:(g + 1) * SSD_STATE, :], xw_b))

    if combine:
        y = ytmp[...] + yb_ref[...] + xc_ref[...] * dsk_ref[...]
        y = y * z_ref[...]
        for g in range(SSD_GROUPS):
            yg = y[:, g * gw:(g + 1) * gw]
            ms = jnp.mean(yg * yg, axis=-1, keepdims=True)
            y_ref[:, g * gw:(g + 1) * gw] = (
                yg * lax.rsqrt(ms + EPS) * nw_ref[:, g * gw:(g + 1) * gw]).astype(BF16)


def _ssd(xc, bm, cm, dt, alog_row, expand, reverse, combine_args=None):
    n_rows = xc.shape[0]
    t_len = SSD_CHUNK
    tile = SSD_CHUNKS_PER_STEP * t_len
    nc = n_rows // tile
    rows = functools.partial(_row_spec, tile, rev_n=nc if reverse else None)
    combine = combine_args is not None
    in_specs = [rows(SSD_INNER), rows(256), rows(256), rows(LANES),
                _const_spec((1, LANES)), _const_spec((LANES, SSD_INNER))]
    args = [xc, bm, cm, dt, alog_row, expand]
    scratch = [pltpu.VMEM((SSD_STATE, SSD_INNER), F32),
               pltpu.VMEM((t_len, t_len), F32),
               pltpu.VMEM((t_len, SSD_INNER // SSD_GROUPS), F32)]
    if combine:
        z, y_bwd, dskip_x, norm_w = combine_args
        in_specs += [rows(SSD_INNER), rows(SSD_INNER),
                     _const_spec((1, SSD_INNER)), _const_spec((1, SSD_INNER))]
        args += [z, y_bwd, dskip_x, norm_w]
        scratch.append(pltpu.VMEM((t_len, SSD_INNER), F32))
    out_dtype = BF16 if combine else F32
    return pl.pallas_call(
        functools.partial(_ssd_body, reverse=reverse, combine=combine),
        grid=(nc,),
        in_specs=in_specs, out_specs=rows(SSD_INNER),
        out_shape=jax.ShapeDtypeStruct((n_rows, SSD_INNER), out_dtype),
        scratch_shapes=scratch,
        compiler_params=_params(), name="ssd_fwd" if combine else "ssd_bwd",
    )(*args)


def _attn_body(sink_ref, q_ref, kp_ref, km_ref, kn_ref, vp_ref, vm_ref, vn_ref, sg_ref, ys_ref,
               x_ref, w_ref, o_ref, k_scr, v_scr, y_scr, yprev_scr, s_scr, *, tq):
    i = pl.program_id(0)
    n = pl.num_programs(0) - 1
    tile = jnp.minimum(i, n - 1)
    blk = ATTN_BLOCK
    nq = tq // blk
    wc = D_MODEL // nq

    @pl.when(i == 0)
    def _():
        y_scr[...] = jnp.zeros_like(y_scr)

    yprev_scr[...] = y_scr[...]
    for scr, p_ref, m_ref, n_ref in ((k_scr, kp_ref, km_ref, kn_ref), (v_scr, vp_ref, vm_ref, vn_ref)):
        scr[0:blk, :] = p_ref[...]
        scr[blk:blk + tq, :] = m_ref[...]
        scr[blk + tq:blk + tq + blk, :] = n_ref[...]

    row2 = lax.broadcasted_iota(jnp.int32, (2 * blk, blk), 0)
    qi = row2 & (blk - 1)
    kj = lax.broadcasted_iota(jnp.int32, (2 * blk, blk), 1)
    first_tile = row2[:, 0:1] < blk
    n_kv = ATTN_KV_HEADS

    for b in range(nq):
        r0 = b * blk
        gb = tile * nq + b
        bias_l = jnp.where((kj >= qi) & (gb > 0), 0.0, NEG)
        bias_r = jnp.where((kj <= qi) & (gb < n * nq - 1), 0.0, NEG)
        for hk in range(n_kv):
            q2 = jnp.concatenate(
                [q_ref[pl.ds(r0, blk), (2 * hk) * LANES:(2 * hk + 1) * LANES],
                 q_ref[pl.ds(r0, blk), (2 * hk + 1) * LANES:(2 * hk + 2) * LANES]], axis=0)
            kz = jnp.concatenate(
                [k_scr[pl.ds(r0, 3 * blk), (2 * hk + half) * LANES:(2 * hk + half + 1) * LANES]
                 for half in range(2)], axis=0)
            s_scr[hk] = lax.dot_general(q2, kz, (((1,), (1,)), ((), ())), preferred_element_type=F32)
        oc = slice(b * wc, (b + 1) * wc)
        o_ref[:, oc] = (x_ref[:, oc] + _dot(ys_ref[...], w_ref[0:SSD_INNER, oc])
                        + _dot(yprev_scr[...], w_ref[SSD_INNER:SSD_INNER + D_MODEL, oc]))
        for hk in range(n_kv):
            out = None
            for half in range(2):
                c = (2 * hk + half) * LANES
                k0 = half * 3 * blk
                s_l = s_scr[hk, :, k0:k0 + blk] + bias_l
                s_m = s_scr[hk, :, k0 + blk:k0 + 2 * blk]
                s_r = s_scr[hk, :, k0 + 2 * blk:k0 + 3 * blk] + bias_r
                sink = jnp.where(first_tile, sink_ref[4 * hk + half], sink_ref[4 * hk + 2 + half])
                m = jnp.max(jnp.maximum(jnp.maximum(s_l, s_m), s_r), axis=-1, keepdims=True)
                m = jnp.maximum(m, sink)
                p_l = jnp.exp(s_l - m)
                p_m = jnp.exp(s_m - m)
                p_r = jnp.exp(s_r - m)
                den = jnp.sum(p_l + p_m + p_r, axis=-1, keepdims=True) + jnp.exp(sink - m)
                p = jnp.concatenate([p_l, p_m, p_r], axis=1).astype(BF16)
                o = _dot(p, v_scr[pl.ds(r0, 3 * blk), c:c + LANES]) * (1.0 / den)
                out = o if out is None else out + o
            for k in range(2):
                t = 2 * hk + k
                y_scr[pl.ds(r0, blk), t * LANES:(t + 1) * LANES] = (
                    out[k * blk:(k + 1) * blk, :]
                    * sg_ref[pl.ds(r0, blk), t * LANES:(t + 1) * LANES]).astype(BF16)


def _attn_out(sink, q, kz, vz, sg, y_ssd, x, w_out):
    n_rows = x.shape[0]
    tq = ATTN_TILE
    blk = ATTN_BLOCK
    per = tq // blk
    nb = n_rows // blk
    n = n_rows // tq

    def cur(i):
        return jnp.minimum(i, n - 1)

    def lag(i):
        return jnp.maximum(i - 1, 0)

    rows_cur = pl.BlockSpec((tq, D_MODEL), lambda i: (cur(i), 0))
    rows_lag = pl.BlockSpec((tq, D_MODEL), lambda i: (lag(i), 0))
    prev = pl.BlockSpec((blk, D_MODEL), lambda i: (jnp.maximum(cur(i) * per - 1, 0), 0))
    nxt = pl.BlockSpec((blk, D_MODEL), lambda i: (jnp.minimum((cur(i) + 1) * per, nb - 1), 0))
    in_specs = [pl.BlockSpec(memory_space=pltpu.SMEM),
                rows_cur, prev, rows_cur, nxt, prev, rows_cur, nxt,
                rows_cur, rows_lag, rows_lag,
                _const_spec(w_out.shape)]
    return pl.pallas_call(
        functools.partial(_attn_body, tq=tq),
        grid=(n + 1,),
        in_specs=in_specs, out_specs=rows_lag,
        out_shape=jax.ShapeDtypeStruct((n_rows, D_MODEL), F32),
        scratch_shapes=[pltpu.VMEM((tq + 2 * blk, D_MODEL), BF16),
                        pltpu.VMEM((tq + 2 * blk, D_MODEL), BF16),
                        pltpu.VMEM((tq, D_MODEL), BF16),
                        pltpu.VMEM((tq, D_MODEL), BF16),
                        pltpu.VMEM((ATTN_KV_HEADS, 2 * blk, 6 * blk), F32)],
        compiler_params=_params(), name="attn_out",
    )(sink, q, kz, kz, kz, vz, vz, vz, sg, y_ssd, x, w_out)


def _odd_in_body(xp_ref, xm_ref, xn_ref, g_ref, w_ref, cw_ref, cb_ref,
                 xc_ref, sg_ref, h_scr, u_scr, *, tm):
    wx_ref = w_ref.at[:, 0:LRU_WIDTH]
    wg_ref = w_ref.at[:, LRU_WIDTH:2 * LRU_WIDTH]
    _fill_normed(xp_ref, xm_ref, xn_ref, g_ref, h_scr, tm)
    h_ext = h_scr[...]
    hm = h_scr[HALO:HALO + tm, :]
    cstep = 256
    for c0 in range(0, LRU_WIDTH, cstep):
        outs = _proj_conv(h_ext, wx_ref, u_scr, cw_ref, cb_ref, tm, c0, cstep)
        for j, acc in enumerate(outs):
            xc_ref[:, c0 + j * LANES:c0 + (j + 1) * LANES] = acc
        sg_ref[:, c0:c0 + cstep] = _silu(_dot(hm, wg_ref[:, c0:c0 + cstep]))


def _odd_in(x, g, w, cw, cb):
    n_rows = x.shape[0]
    tm = IN_TILE_ODD
    rows = functools.partial(_row_spec, tm)
    in_specs = _halo_specs(tm, n_rows) + [
        _const_spec((1, D_MODEL)), _const_spec(w.shape),
        _const_spec(cw.shape), _const_spec(cb.shape)]
    return pl.pallas_call(
        functools.partial(_odd_in_body, tm=tm),
        grid=(n_rows // tm,),
        in_specs=in_specs, out_specs=(rows(LRU_WIDTH), rows(LRU_WIDTH)),
        out_shape=(jax.ShapeDtypeStruct((n_rows, LRU_WIDTH), F32),
                   jax.ShapeDtypeStruct((n_rows, LRU_WIDTH), F32)),
        scratch_shapes=[pltpu.VMEM((tm + 2 * HALO, D_MODEL), BF16),
                        pltpu.VMEM((LRU_WIDTH // LANES, tm + 2 * HALO, LANES), F32)],
        compiler_params=_params(), name="odd_in",
    )(x, x, x, g, w, cw, cb)


def _lru_body(*refs, reverse, combine, final):
    if combine:
        if final:
            (xc_ref, w_ref, ba_ref, bx_ref, lam_ref, hb_ref, sg_ref, x_ref, wo_ref, fw_ref,
             o_ref, a_scr, u_scr, carry, n_scr, y_scr) = refs
        else:
            (xc_ref, w_ref, ba_ref, bx_ref, lam_ref, hb_ref, sg_ref, x_ref, wo_ref,
             o_ref, a_scr, u_scr, carry, n_scr, y_scr) = refs
    else:
        xc_ref, w_ref, ba_ref, bx_ref, lam_ref, o_ref, a_scr, u_scr, carry = refs
    tm, sub, pitch = LRU_TILE, LRU_SUB, LRU_PITCH
    vregs_per_sub = sub // SUBLANES
    grp_per_blk = LRU_BLOCK // LANES
    half_w = LRU_HALF_GROUPS * LANES

    @pl.when(pl.program_id(0) == 0)
    def _():
        carry[...] = jnp.zeros_like(carry)

    half_rate = (-0.5 * LRU_C) * jax.nn.softplus(-lam_ref[...])
    steps = list(range(sub))
    order = list(range(SUBLANES))
    if reverse:
        steps = steps[::-1]
        order = order[::-1]

    for blk in range(LRU_BLOCKS):
        c0 = blk * LRU_BLOCK
        xb = xc_ref[:, c0:c0 + LRU_BLOCK]
        ri = _dot(xb.astype(BF16), w_ref[blk])
        hr = half_rate[:, c0:c0 + LRU_BLOCK]
        log_a = hr + hr * jnp.tanh(0.5 * (ri[:, :LRU_BLOCK] + ba_ref[:, c0:c0 + LRU_BLOCK]))
        hx = 0.5 * xb
        gx = hx + hx * jnp.tanh(0.5 * (ri[:, LRU_BLOCK:] + bx_ref[:, c0:c0 + LRU_BLOCK]))
        a = jnp.exp(log_a)
        th = jnp.tanh(log_a)
        num = -2.0 * th
        root = jnp.where(num > 0.0, num * lax.rsqrt(num), 0.0) * lax.rsqrt(1.0 - th)
        u = root * gx
        for half in range(grp_per_blk):
            g = blk * grp_per_blk + half
            ls = slice(half * LANES, (half + 1) * LANES)
            for k in range(tm // SUBLANES):
                s = k // vregs_per_sub
                jj = (k % vregs_per_sub) * SUBLANES
                dst = pl.ds(jj * SUBLANES + s, SUBLANES, stride=SUBLANES)
                a_scr[g, dst, :] = a[k * SUBLANES:(k + 1) * SUBLANES, ls]
                u_scr[g, dst, :] = u[k * SUBLANES:(k + 1) * SUBLANES, ls]

    def rows_t(j):
        return slice(j * SUBLANES, (j + 1) * SUBLANES)

    for hf in range(LRU_WIDTH // half_w):
        gs = slice(hf * LRU_HALF_GROUPS, (hf + 1) * LRU_HALF_GROUPS)
        h = u_scr[gs, rows_t(steps[0]), :]
        p = a_scr[gs, rows_t(steps[0]), :]
        for j in steps[1:]:
            aj = a_scr[gs, rows_t(j), :]
            h = aj * h + u_scr[gs, rows_t(j), :]
            p = aj * p
            u_scr[gs, rows_t(j), :] = h
            a_scr[gs, rows_t(j), :] = p

        c = carry[gs]
        carry_in = [None] * SUBLANES
        for s in order:
            carry_in[s] = c
            c = p[:, s:s + 1, :] * c + h[:, s:s + 1, :]
        carry[gs] = c
        cmat = jnp.concatenate(carry_in, axis=1)

        for j in range(sub):
            hj = u_scr[gs, rows_t(j), :] + a_scr[gs, rows_t(j), :] * cmat
            for g in range(LRU_HALF_GROUPS):
                cs_ = slice(hf * half_w + g * LANES, hf * half_w + (g + 1) * LANES)
                if combine:
                    n_scr[g, pl.ds(j, SUBLANES, stride=pitch), :] = hj[g] + hb_ref[rows_t(j), cs_]
                else:
                    o_ref[rows_t(j), cs_] = hj[g]

        if combine:
            for s in range(SUBLANES):
                rs = slice(s * sub, (s + 1) * sub)
                for g in range(LRU_HALF_GROUPS):
                    cs_ = slice(hf * half_w + g * LANES, hf * half_w + (g + 1) * LANES)
                    y_scr[rs, cs_] = (n_scr[g, s * pitch:s * pitch + sub, :] * sg_ref[rs, cs_]).astype(BF16)

    if combine:
        out = x_ref[...] + _dot(y_scr[...], wo_ref[...])
        if final:
            out = _rms(out, fw_ref[...])
        o_ref[...] = out


def _lru(xc, w_gates, b_a, b_x, lam, reverse, combine_args=None, final_w=None):
    n_rows = xc.shape[0]
    tm = LRU_TILE
    n = n_rows // tm
    combine = combine_args is not None
    final = final_w is not None
    rows = functools.partial(_row_spec, tm, rev_n=n if reverse else None)
    in_specs = [rows(LRU_WIDTH), _const_spec(w_gates.shape), _const_spec((1, LRU_WIDTH)),
                _const_spec((1, LRU_WIDTH)), _const_spec((1, LRU_WIDTH))]
    args = [xc, w_gates, b_a, b_x, lam]
    n_grp = LRU_WIDTH // LANES
    scratch = [pltpu.VMEM((n_grp, tm, LANES), F32),
               pltpu.VMEM((n_grp, tm, LANES), F32),
               pltpu.VMEM((n_grp, 1, LANES), F32)]
    if combine:
        h_bwd, sg, x, w_out = combine_args
        in_specs += [rows(LRU_WIDTH), rows(LRU_WIDTH), rows(D_MODEL), _const_spec(w_out.shape)]
        args += [h_bwd, sg, x, w_out]
        if final:
            in_specs.append(_const_spec((1, D_MODEL)))
            args.append(final_w)
        scratch.append(pltpu.VMEM((LRU_HALF_GROUPS, SUBLANES * LRU_PITCH, LANES), F32))
        scratch.append(pltpu.VMEM((tm, LRU_WIDTH), BF16))
        out_w = D_MODEL
    else:
        out_w = LRU_WIDTH
    return pl.pallas_call(
        functools.partial(_lru_body, reverse=reverse, combine=combine, final=final),
        grid=(n,),
        in_specs=in_specs, out_specs=rows(out_w),
        out_shape=jax.ShapeDtypeStruct((n_rows, out_w), F32),
        scratch_shapes=scratch,
        compiler_params=_params(), name="lru_fwd" if combine else "lru_bwd",
    )(*args)


def _rope_tables(n_rows):
    half = ROPE_DIM // 2
    inv = ROPE_THETA ** (-jnp.arange(0, ROPE_DIM, 2, dtype=F32) / ROPE_DIM)
    d = jnp.arange(LANES) % ATTN_HEAD_DIM
    lo = (d < half)[None, :]
    hi = ((d >= half) & (d < ROPE_DIM))[None, :]
    inv_lane = jnp.where(lo | hi, jnp.tile(inv, LANES // half)[None, :], 0.0)
    ang = jnp.arange(n_rows, dtype=F32)[:, None] * inv_lane
    cos_t = jnp.cos(ang)
    sin = jnp.sin(ang)
    return cos_t, jnp.where(lo, -sin, 0.0), jnp.where(hi, sin, 0.0)


def _pad_lanes(v):
    return jnp.pad(v.astype(F32), (0, LANES - v.shape[0]))[None, :]


def _even_layer(x, g_norm, w_in, conv_w, conv_b, dt_bias, a_log, d_skip, ssd_norm_w, sink, w_out, rope):
    n_dt = 2 * SSD_HEADS
    dt0 = SSD_INNER + SSD_CONV_CH
    w_packed = jnp.concatenate(
        [w_in[:, :dt0], w_in[:, dt0 + n_dt:], w_in[:, dt0:dt0 + n_dt],
         jnp.zeros((D_MODEL, LANES - n_dt), w_in.dtype)], axis=1).astype(BF16)
    dtb = _pad_lanes(dt_bias.reshape(-1))
    z, xc, bm, cm, dt, q, kz, vz, sg = _even_in(
        x, g_norm[None, :], w_packed, conv_w, conv_b[None, :], dtb, *rope)

    lane = jnp.arange(LANES)[:, None]
    head = (jnp.arange(SSD_INNER) // SSD_HEAD_DIM)[None, :]
    alog_row = _pad_lanes(a_log.reshape(-1))
    y_bwd = _ssd(xc, bm, cm, dt, alog_row, (lane == head + SSD_HEADS).astype(BF16), reverse=True)
    dskip_x = jnp.repeat(d_skip.astype(F32), SSD_HEAD_DIM)[None, :]
    y_ssd = _ssd(xc, bm, cm, dt, alog_row, (lane == head).astype(BF16), reverse=False,
                 combine_args=(z, y_bwd, dskip_x, ssd_norm_w[None, :]))

    return _attn_out(sink.astype(F32), q, kz, vz, sg, y_ssd, x, w_out.astype(BF16))


def _odd_layer(x, g_norm, w_in, conv_w, conv_b, w_a, b_a, w_x, b_x, lam, w_out, final_w):
    xc, sg = _odd_in(x, g_norm[None, :], w_in.astype(BF16), conv_w, conv_b[None, :])
    w_gates = jnp.concatenate([w_a, w_x], axis=-1).astype(BF16)
    h_bwd = _lru(xc, w_gates[1], b_a[1][None, :], b_x[1][None, :], lam[1][None, :], reverse=True)
    return _lru(xc, w_gates[0], b_a[0][None, :], b_x[0][None, :], lam[0][None, :], reverse=False,
                combine_args=(h_bwd, sg, x, w_out.astype(BF16)),
                final_w=None if final_w is None else final_w[None, :])


def kernel(x, norm_w, final_norm_w, ev_w_in, ev_conv_w, ev_conv_b, ev_dt_bias, ev_a_log, ev_d_skip, ev_ssd_norm_w, ev_sink, ev_w_out, od_w_in, od_conv_w, od_conv_b, od_w_a, od_b_a, od_w_x, od_b_x, od_lambda, od_w_out):
    batch, n_rows, _ = x.shape
    depth = norm_w.shape[0]
    assert batch == 1 and depth % 2 == 0
    rope = _rope_tables(n_rows)
    h = x[0]
    for layer in range(depth):
        j = layer // 2
        if layer % 2 == 0:
            h = _even_layer(h, norm_w[layer], ev_w_in[j], ev_conv_w[j], ev_conv_b[j], ev_dt_bias[j],
                            ev_a_log[j], ev_d_skip[j], ev_ssd_norm_w[j], ev_sink[j], ev_w_out[j], rope)
        else:
            h = _odd_layer(h, norm_w[layer], od_w_in[j], od_conv_w[j], od_conv_b[j], od_w_a[j],
                           od_b_a[j], od_w_x[j], od_b_x[j], od_lambda[j], od_w_out[j],
                           final_norm_w if layer == depth - 1 else None)
    return h[None]
```

```python
import functools

import jax
import jax.numpy as jnp
from jax import lax
from jax.experimental import pallas as pl
from jax.experimental.pallas import tpu as pltpu

F32 = jnp.float32
BF16 = jnp.bfloat16

D_MODEL = 1024
EPS = 1e-6
LANES = 128
SUBLANES = 8
HALO = 16
CONV_W = 4
CONV_LEFT = 2

SSD_HEADS = 16
SSD_HEAD_DIM = 64
SSD_STATE = 128
SSD_GROUPS = 2
SSD_CHUNK = 256
SSD_CHUNKS_PER_STEP = 4
SSD_INNER = 1024
SSD_CONV_CH = 1536

EVEN_W_SIZES = (1024, 1536, 1024, 256, 256, 1024, 128)

ATTN_HEADS = 16
ATTN_KV_HEADS = 4
ATTN_HEAD_DIM = 64
ATTN_BLOCK = 128
ROPE_THETA = 500000.0
ROPE_DIM = 16

LRU_WIDTH = 2048
LRU_BLOCK = 256
LRU_BLOCKS = 8
LRU_C = 8.0

LOG2E = 1.4426950408889634
NEG = -1e30
VMEM_LIMIT = 56 * 1024 * 1024

IN_TILE_EVEN = 512
IN_TILE_ODD = 512
ATTN_TILE = 512
LRU_TILE = 512
LRU_SUB = LRU_TILE // SUBLANES
LRU_PITCH = LRU_SUB + SUBLANES
LRU_HALF_GROUPS = 8


def _dot(a, b):
    return jnp.dot(a, b, preferred_element_type=F32)


def _split3(v):
    hi = v.astype(BF16)
    r1 = v - hi.astype(F32)
    mid = r1.astype(BF16)
    lo = (r1 - mid.astype(F32)).astype(BF16)
    return hi, mid, lo


def _dot3(*ops, rhs_split=True):
    if rhs_split:
        m, parts = ops[0], ops[1:]
        return _dot(m, parts[0]) + _dot(m, parts[1]) + _dot(m, parts[2])
    parts, m = ops[:3], ops[3]
    return _dot(parts[0], m) + _dot(parts[1], m) + _dot(parts[2], m)


def _rms(x, g):
    ms = jnp.mean(x * x, axis=-1, keepdims=True)
    return x * lax.rsqrt(ms + EPS) * g


def _sigmoid(x):
    return 0.5 * jnp.tanh(0.5 * x) + 0.5


def _silu(x):
    h = 0.5 * x
    return h + h * jnp.tanh(h)


def _params(n_axes=1):
    return pltpu.CompilerParams(dimension_semantics=("arbitrary",) * n_axes,
                                vmem_limit_bytes=VMEM_LIMIT)


def _const_spec(shape):
    nd = len(shape)
    return pl.BlockSpec(shape, lambda i: (0,) * nd, pipeline_mode=pl.Buffered(1))


def _row_spec(tm, width, rev_n=None):
    if rev_n is None:
        return pl.BlockSpec((tm, width), lambda i: (i, 0))
    return pl.BlockSpec((tm, width), lambda i: (rev_n - 1 - i, 0))


def _halo_specs(tm, n_rows):
    hb = tm // HALO
    nh = n_rows // HALO
    prev = pl.BlockSpec((HALO, D_MODEL), lambda i: (jnp.maximum(i * hb - 1, 0), 0))
    main = pl.BlockSpec((tm, D_MODEL), lambda i: (i, 0))
    nxt = pl.BlockSpec((HALO, D_MODEL), lambda i: (jnp.minimum((i + 1) * hb, nh - 1), 0))
    return [prev, main, nxt]


def _fill_normed(xp_ref, xm_ref, xn_ref, g_ref, h_scr, tm):
    i = pl.program_id(0)
    n = pl.num_programs(0)
    g = g_ref[...]
    hp = _rms(xp_ref[...], g)
    hn = _rms(xn_ref[...], g)
    h_scr[0:HALO, :] = jnp.where(i > 0, hp, 0.0).astype(BF16)
    h_scr[HALO:HALO + tm, :] = _rms(xm_ref[...], g).astype(BF16)
    h_scr[HALO + tm:HALO + tm + HALO, :] = jnp.where(i < n - 1, hn, 0.0).astype(BF16)


def _proj_conv(h_ext, w_ref, u_scr, cw_ref, cb_ref, tm, c0, width):
    u = _dot(h_ext, w_ref[:, c0:c0 + width])
    base = HALO - CONV_LEFT
    outs = []
    for j in range(width // LANES):
        slab = c0 // LANES + j
        u_scr[slab] = u[:, j * LANES:(j + 1) * LANES]
        cs_ = slice(c0 + j * LANES, c0 + (j + 1) * LANES)
        acc = cb_ref[:, cs_] + cw_ref[0:1, cs_] * u_scr[slab, base:base + tm, :]
        for k in range(1, CONV_W):
            acc = acc + cw_ref[k:k + 1, cs_] * u_scr[slab, base + k:base + k + tm, :]
        outs.append(acc)
    return outs


def _even_in_body(xp_ref, xm_ref, xn_ref, g_ref, w_ref, cw_ref, cb_ref, dtb_ref, cos_ref, sa_ref, sb_ref,
                  z_ref, xc_ref, b_ref, c_ref, dt_ref, q_ref, kz_ref, vz_ref, sg_ref,
                  h_scr, u_scr, *, tm):
    cuts = [0]
    for width in EVEN_W_SIZES:
        cuts.append(cuts[-1] + width)
    wz_ref, wx_ref, wq_ref, wk_ref, wv_ref, wg_ref, wdt_ref = (
        w_ref.at[:, cuts[k]:cuts[k + 1]] for k in range(len(EVEN_W_SIZES)))
    _fill_normed(xp_ref, xm_ref, xn_ref, g_ref, h_scr, tm)
    h_ext = h_scr[...]
    hm = h_scr[HALO:HALO + tm, :]
    cstep = 256
    cos_t = cos_ref[...]
    sin_a = sa_ref[...]
    sin_b = sb_ref[...]
    low = lax.broadcasted_iota(jnp.int32, (tm, LANES), 1) < ATTN_HEAD_DIM

    def rope(t):
        return (t * cos_t + pltpu.roll(t, LANES - ROPE_DIM // 2, 1) * sin_a
                + pltpu.roll(t, ROPE_DIM // 2, 1) * sin_b)

    def do_xbc(c0):
        outs = _proj_conv(h_ext, wx_ref, u_scr, cw_ref, cb_ref, tm, c0, cstep)
        for j, acc in enumerate(outs):
            y = _silu(acc)
            c = c0 + j * LANES
            if c < SSD_INNER:
                xc_ref[:, c:c + LANES] = y
            elif c < SSD_INNER + 256:
                b_ref[:, c - SSD_INNER:c - SSD_INNER + LANES] = y
            else:
                c_ref[:, c - SSD_INNER - 256:c - SSD_INNER - 256 + LANES] = y

    def do_z(c0):
        z_ref[:, c0:c0 + cstep] = _silu(_dot(hm, wz_ref[:, c0:c0 + cstep]))

    def do_gate(c0):
        sg_ref[:, c0:c0 + cstep] = _silu(_dot(hm, wg_ref[:, c0:c0 + cstep]))

    def do_q(c0):
        qf = _dot(hm, wq_ref[:, c0:c0 + cstep]) * (ATTN_HEAD_DIM ** -0.5)
        for j in range(cstep // LANES):
            q_ref[:, c0 + j * LANES:c0 + (j + 1) * LANES] = rope(qf[:, j * LANES:(j + 1) * LANES]).astype(BF16)

    def do_kv(w_ref, dst, rotary):
        f = _dot(hm, w_ref[...])
        for j in range(2):
            a = f[:, j * LANES:(j + 1) * LANES]
            if rotary:
                a = rope(a)
            r = pltpu.roll(a, ATTN_HEAD_DIM, 1)
            c = 4 * j * LANES
            dst[:, c:c + LANES] = jnp.where(low, a, 0.0).astype(BF16)
            dst[:, c + LANES:c + 2 * LANES] = jnp.where(low, 0.0, r).astype(BF16)
            dst[:, c + 2 * LANES:c + 3 * LANES] = jnp.where(low, r, 0.0).astype(BF16)
            dst[:, c + 3 * LANES:c + 4 * LANES] = jnp.where(low, 0.0, a).astype(BF16)

    for n in range(4):
        do_xbc(n * cstep)
        do_z(n * cstep)
        do_q(n * cstep)
        do_gate(n * cstep)
        if n == 0:
            do_xbc(4 * cstep)
        elif n == 1:
            do_xbc(5 * cstep)
        elif n == 2:
            do_kv(wk_ref, kz_ref, True)
        else:
            do_kv(wv_ref, vz_ref, False)
    dt_ref[...] = jax.nn.softplus(_dot(hm, wdt_ref[...]) + dtb_ref[...])


def _even_in(x, g, w_packed, cw, cb, dtb, cos_t, sin_a, sin_b):
    n_rows = x.shape[0]
    tm = IN_TILE_EVEN
    rows = functools.partial(_row_spec, tm)
    in_specs = _halo_specs(tm, n_rows) + [
        _const_spec((1, D_MODEL)), _const_spec(w_packed.shape),
        _const_spec(cw.shape), _const_spec(cb.shape), _const_spec(dtb.shape),
        rows(LANES), rows(LANES), rows(LANES),
    ]
    out_shape = (
        jax.ShapeDtypeStruct((n_rows, SSD_INNER), F32),
        jax.ShapeDtypeStruct((n_rows, SSD_INNER), F32),
        jax.ShapeDtypeStruct((n_rows, 256), F32),
        jax.ShapeDtypeStruct((n_rows, 256), F32),
        jax.ShapeDtypeStruct((n_rows, LANES), F32),
        jax.ShapeDtypeStruct((n_rows, D_MODEL), BF16),
        jax.ShapeDtypeStruct((n_rows, D_MODEL), BF16),
        jax.ShapeDtypeStruct((n_rows, D_MODEL), BF16),
        jax.ShapeDtypeStruct((n_rows, D_MODEL), F32),
    )
    out_specs = (rows(SSD_INNER), rows(SSD_INNER), rows(256), rows(256), rows(LANES),
                 rows(D_MODEL), rows(D_MODEL), rows(D_MODEL), rows(D_MODEL))
    return pl.pallas_call(
        functools.partial(_even_in_body, tm=tm),
        grid=(n_rows // tm,),
        in_specs=in_specs, out_specs=out_specs, out_shape=out_shape,
        scratch_shapes=[pltpu.VMEM((tm + 2 * HALO, D_MODEL), BF16),
                        pltpu.VMEM((SSD_CONV_CH // LANES, tm + 2 * HALO, LANES), F32)],
        compiler_params=_params(), name="even_in",
    )(x, x, x, g, w_packed, cw, cb, dtb, cos_t, sin_a, sin_b)


def _ssd_body(*refs, reverse, combine):
    if combine:
        (xc_ref, b_ref, c_ref, dt_ref, alog_ref, e_ref, z_ref, yb_ref, dsk_ref, nw_ref,
         y_ref, state, cb_scr, yoff_scr, ytmp) = refs
    else:
        xc_ref, b_ref, c_ref, dt_ref, alog_ref, e_ref, y_ref, state, cb_scr, yoff_scr = refs
        z_ref = yb_ref = dsk_ref = nw_ref = ytmp = None

    @pl.when(pl.program_id(0) == 0)
    def _():
        state[...] = jnp.zeros_like(state)

    chunks = list(range(SSD_CHUNKS_PER_STEP))
    if reverse:
        chunks = chunks[::-1]
    for ci in chunks:
        rs = pl.ds(ci * SSD_CHUNK, SSD_CHUNK)
        y_view = y_ref.at[rs]
        _ssd_chunk(xc_ref.at[rs], b_ref.at[rs], c_ref.at[rs], dt_ref.at[rs], alog_ref, e_ref,
                   z_ref.at[rs] if combine else None, yb_ref.at[rs] if combine else None,
                   dsk_ref, nw_ref, y_view, state, cb_scr, yoff_scr,
                   ytmp if combine else y_view, reverse=reverse, combine=combine)


def _ssd_chunk(xc_ref, b_ref, c_ref, dt_ref, alog_ref, e_ref, z_ref, yb_ref, dsk_ref, nw_ref,
               y_ref, state, cb_scr, yoff_scr, ytmp, *, reverse, combine):
    t_len = SSD_CHUNK
    off = SSD_HEADS if reverse else 0

    row = lax.broadcasted_iota(jnp.int32, (t_len, t_len), 0)
    col = lax.broadcasted_iota(jnp.int32, (t_len, t_len), 1)
    mask = (col >= row) if reverse else (col <= row)
    tri = jnp.where(mask, 1.0, 0.0).astype(BF16)

    lane = lax.broadcasted_iota(jnp.int32, (1, LANES), 1)
    in_dir = (lane >= off) & (lane < off + SSD_HEADS)
    a_rate = jnp.where(in_dir, -LOG2E * jnp.exp(alog_ref[...]), 0.0)

    dt = dt_ref[...]
    cs = _dot3(tri, *_split3(dt * a_rate))
    log_dt = jnp.log2(dt)
    cs_t = (cs - log_dt).T
    last = 0 if reverse else t_len - 1
    cs_last = cs[last:last + 1, :]
    w_hi, w_mid, _ = _split3(jnp.exp2(cs_last - cs + log_dt))
    expand = e_ref[...]
    wdt_x = _dot(w_hi, expand) + _dot(w_mid, expand)

    b_f = b_ref[...]
    c_b = c_ref[...].astype(BF16)
    b_b = b_f.astype(BF16)
    b_t = b_f.T.astype(BF16)
    low = lax.broadcasted_iota(jnp.int32, (t_len, LANES), 1) < SSD_HEAD_DIM

    hb = t_len // 2
    key_blocks = ((0, 1), (1,)) if reverse else ((0,), (0, 1))
    diag_mask = mask[0:hb, 0:hb]
    low_hb = lax.broadcasted_iota(jnp.int32, (hb, LANES), 1) < SSD_HEAD_DIM

    gw = SSD_INNER // SSD_GROUPS
    for g in range(SSD_GROUPS):
        c_g = c_b[:, g * SSD_STATE:(g + 1) * SSD_STATE]
        b_g = b_b[:, g * SSD_STATE:(g + 1) * SSD_STATE]
        cb_scr[...] = lax.dot_general(c_g, b_g, (((1,), (1,)), ((), ())), preferred_element_type=F32)
        s_g = state[:, g * gw:(g + 1) * gw]
        yoff_scr[...] = _dot(c_g, s_g.astype(BF16))
        edge_decay = []
        for tt in range(gw // LANES):
            t = g * (gw // LANES) + tt
            ts = slice(t * LANES, (t + 1) * LANES)
            xt = xc_ref[:, ts]
            x_halves = (jnp.where(low, xt, 0.0).astype(BF16), jnp.where(low, 0.0, xt).astype(BF16))
            for rb in range(2):
                rs = slice(rb * hb, (rb + 1) * hb)
                cols = [jnp.broadcast_to(cs[rs, off + 2 * t + half:off + 2 * t + half + 1], (hb, LANES))
                        for half in range(2)]
                e_tile = jnp.where(low_hb, jnp.exp2(cols[0]), jnp.exp2(cols[1]))
                if rb * hb <= last < (rb + 1) * hb:
                    edge_decay.append(e_tile[last - rb * hb:last - rb * hb + 1, :])
                acc = yoff_scr[rs, tt * LANES:(tt + 1) * LANES] * e_tile
                for half in range(2):
                    ln = off + 2 * t + half
                    pieces = []
                    for kb in key_blocks[rb]:
                        ks = slice(kb * hb, (kb + 1) * hb)
                        seg = cols[half] - cs_t[ln:ln + 1, ks]
                        if kb == rb:
                            seg = jnp.where(diag_mask, seg, NEG)
                        pieces.append((cb_scr[rs, ks] * jnp.exp2(seg)).astype(BF16))
                    m = pieces[0] if len(pieces) == 1 else jnp.concatenate(pieces, axis=1)
                    k0, k1 = key_blocks[rb][0] * hb, (key_blocks[rb][-1] + 1) * hb
                    acc = acc + _dot(m, x_halves[half][k0:k1, :])
                ytmp[rs, ts] = acc.astype(ytmp.dtype)
        gs = slice(g * gw, (g + 1) * gw)
        xw_b = (xc_ref[:, gs] * wdt_x[:, gs]).astype(BF16)
        state[:, gs] = (jnp.concatenate(edge_decay, axis=1) * s_g
                        + _dot(b_t[g * SSD_STATE:(g + 1) * SSD_STATE, :], xw_b))

    if combine:
        y = ytmp[...] + yb_ref[...] + xc_ref[...] * dsk_ref[...]
        y = y * z_ref[...]
        for g in range(SSD_GROUPS):
            yg = y[:, g * gw:(g + 1) * gw]
            ms = jnp.mean(yg * yg, axis=-1, keepdims=True)
            y_ref[:, g * gw:(g + 1) * gw] = (
                yg * lax.rsqrt(ms + EPS) * nw_ref[:, g * gw:(g + 1) * gw]).astype(BF16)


def _ssd(xc, bm, cm, dt, alog_row, expand, reverse, combine_args=None):
    n_rows = xc.shape[0]
    t_len = SSD_CHUNK
    tile = SSD_CHUNKS_PER_STEP * t_len
    nc = n_rows // tile
    rows = functools.partial(_row_spec, tile, rev_n=nc if reverse else None)
    combine = combine_args is not None
    in_specs = [rows(SSD_INNER), rows(256), rows(256), rows(LANES),
                _const_spec((1, LANES)), _const_spec((LANES, SSD_INNER))]
    args = [xc, bm, cm, dt, alog_row, expand]
    scratch = [pltpu.VMEM((SSD_STATE, SSD_INNER), F32),
               pltpu.VMEM((t_len, t_len), F32),
               pltpu.VMEM((t_len, SSD_INNER // SSD_GROUPS), F32)]
    if combine:
        z, y_bwd, dskip_x, norm_w = combine_args
        in_specs += [rows(SSD_INNER), rows(SSD_INNER),
                     _const_spec((1, SSD_INNER)), _const_spec((1, SSD_INNER))]
        args += [z, y_bwd, dskip_x, norm_w]
        scratch.append(pltpu.VMEM((t_len, SSD_INNER), F32))
    out_dtype = BF16 if combine else F32
    return pl.pallas_call(
        functools.partial(_ssd_body, reverse=reverse, combine=combine),
        grid=(nc,),
        in_specs=in_specs, out_specs=rows(SSD_INNER),
        out_shape=jax.ShapeDtypeStruct((n_rows, SSD_INNER), out_dtype),
        scratch_shapes=scratch,
        compiler_params=_params(), name="ssd_fwd" if combine else "ssd_bwd",
    )(*args)


def _attn_body(sink_ref, q_ref, kp_ref, km_ref, kn_ref, vp_ref, vm_ref, vn_ref, sg_ref, ys_ref,
               x_ref, w_ref, o_ref, k_scr, v_scr, y_scr, yprev_scr, s_scr, *, tq):
    i = pl.program_id(0)
    n = pl.num_programs(0) - 1
    tile = jnp.minimum(i, n - 1)
    blk = ATTN_BLOCK
    nq = tq // blk
    wc = D_MODEL // nq

    @pl.when(i == 0)
    def _():
        y_scr[...] = jnp.zeros_like(y_scr)

    yprev_scr[...] = y_scr[...]
    for scr, p_ref, m_ref, n_ref in ((k_scr, kp_ref, km_ref, kn_ref), (v_scr, vp_ref, vm_ref, vn_ref)):
        scr[0:blk, :] = p_ref[...]
        scr[blk:blk + tq, :] = m_ref[...]
        scr[blk + tq:blk + tq + blk, :] = n_ref[...]

    row2 = lax.broadcasted_iota(jnp.int32, (2 * blk, blk), 0)
    qi = row2 & (blk - 1)
    kj = lax.broadcasted_iota(jnp.int32, (2 * blk, blk), 1)
    first_tile = row2[:, 0:1] < blk
    n_kv = ATTN_KV_HEADS

    for b in range(nq):
        r0 = b * blk
        gb = tile * nq + b
        bias_l = jnp.where((kj >= qi) & (gb > 0), 0.0, NEG)
        bias_r = jnp.where((kj <= qi) & (gb < n * nq - 1), 0.0, NEG)
        for hk in range(n_kv):
            q2 = jnp.concatenate(
                [q_ref[pl.ds(r0, blk), (2 * hk) * LANES:(2 * hk + 1) * LANES],
                 q_ref[pl.ds(r0, blk), (2 * hk + 1) * LANES:(2 * hk + 2) * LANES]], axis=0)
            kz = jnp.concatenate(
                [k_scr[pl.ds(r0, 3 * blk), (2 * hk + half) * LANES:(2 * hk + half + 1) * LANES]
                 for half in range(2)], axis=0)
            s_scr[hk] = lax.dot_general(q2, kz, (((1,), (1,)), ((), ())), preferred_element_type=F32)
        oc = slice(b * wc, (b + 1) * wc)
        o_ref[:, oc] = (x_ref[:, oc] + _dot(ys_ref[...], w_ref[0:SSD_INNER, oc])
                        + _dot(yprev_scr[...], w_ref[SSD_INNER:SSD_INNER + D_MODEL, oc]))
        for hk in range(n_kv):
            out = None
            for half in range(2):
                c = (2 * hk + half) * LANES
                k0 = half * 3 * blk
                s_l = s_scr[hk, :, k0:k0 + blk] + bias_l
                s_m = s_scr[hk, :, k0 + blk:k0 + 2 * blk]
                s_r = s_scr[hk, :, k0 + 2 * blk:k0 + 3 * blk] + bias_r
                sink = jnp.where(first_tile, sink_ref[4 * hk + half], sink_ref[4 * hk + 2 + half])
                m = jnp.max(jnp.maximum(jnp.maximum(s_l, s_m), s_r), axis=-1, keepdims=True)
                m = jnp.maximum(m, sink)
                p_l = jnp.exp(s_l - m)
                p_m = jnp.exp(s_m - m)
                p_r = jnp.exp(s_r - m)
                den = jnp.sum(p_l + p_m + p_r, axis=-1, keepdims=True) + jnp.exp(sink - m)
                p = jnp.concatenate([p_l, p_m, p_r], axis=1).astype(BF16)
                o = _dot(p, v_scr[pl.ds(r0, 3 * blk), c:c + LANES]) * (1.0 / den)
                out = o if out is None else out + o
            for k in range(2):
                t = 2 * hk + k
                y_scr[pl.ds(r0, blk), t * LANES:(t + 1) * LANES] = (
                    out[k * blk:(k + 1) * blk, :]
                    * sg_ref[pl.ds(r0, blk), t * LANES:(t + 1) * LANES]).astype(BF16)


def _attn_out(sink, q, kz, vz, sg, y_ssd, x, w_out):
    n_rows = x.shape[0]
    tq = ATTN_TILE
    blk = ATTN_BLOCK
    per = tq // blk
    nb = n_rows // blk
    n = n_rows // tq

    def cur(i):
        return jnp.minimum(i, n - 1)

    def lag(i):
        return jnp.maximum(i - 1, 0)

    rows_cur = pl.BlockSpec((tq, D_MODEL), lambda i: (cur(i), 0))
    rows_lag = pl.BlockSpec((tq, D_MODEL), lambda i: (lag(i), 0))
    prev = pl.BlockSpec((blk, D_MODEL), lambda i: (jnp.maximum(cur(i) * per - 1, 0), 0))
    nxt = pl.BlockSpec((blk, D_MODEL), lambda i: (jnp.minimum((cur(i) + 1) * per, nb - 1), 0))
    in_specs = [pl.BlockSpec(memory_space=pltpu.SMEM),
                rows_cur, prev, rows_cur, nxt, prev, rows_cur, nxt,
                rows_cur, rows_lag, rows_lag,
                _const_spec(w_out.shape)]
    return pl.pallas_call(
        functools.partial(_attn_body, tq=tq),
        grid=(n + 1,),
        in_specs=in_specs, out_specs=rows_lag,
        out_shape=jax.ShapeDtypeStruct((n_rows, D_MODEL), F32),
        scratch_shapes=[pltpu.VMEM((tq + 2 * blk, D_MODEL), BF16),
                        pltpu.VMEM((tq + 2 * blk, D_MODEL), BF16),
                        pltpu.VMEM((tq, D_MODEL), BF16),
                        pltpu.VMEM((tq, D_MODEL), BF16),
                        pltpu.VMEM((ATTN_KV_HEADS, 2 * blk, 6 * blk), F32)],
        compiler_params=_params(), name="attn_out",
    )(sink, q, kz, kz, kz, vz, vz, vz, sg, y_ssd, x, w_out)


def _odd_in_body(xp_ref, xm_ref, xn_ref, g_ref, w_ref, cw_ref, cb_ref,
                 xc_ref, sg_ref, h_scr, u_scr, *, tm):
    wx_ref = w_ref.at[:, 0:LRU_WIDTH]
    wg_ref = w_ref.at[:, LRU_WIDTH:2 * LRU_WIDTH]
    _fill_normed(xp_ref, xm_ref, xn_ref, g_ref, h_scr, tm)
    h_ext = h_scr[...]
    hm = h_scr[HALO:HALO + tm, :]
    cstep = 256
    for c0 in range(0, LRU_WIDTH, cstep):
        outs = _proj_conv(h_ext, wx_ref, u_scr, cw_ref, cb_ref, tm, c0, cstep)
        for j, acc in enumerate(outs):
            xc_ref[:, c0 + j * LANES:c0 + (j + 1) * LANES] = acc
        sg_ref[:, c0:c0 + cstep] = _silu(_dot(hm, wg_ref[:, c0:c0 + cstep]))


def _odd_in(x, g, w, cw, cb):
    n_rows = x.shape[0]
    tm = IN_TILE_ODD
    rows = functools.partial(_row_spec, tm)
    in_specs = _halo_specs(tm, n_rows) + [
        _const_spec((1, D_MODEL)), _const_spec(w.shape),
        _const_spec(cw.shape), _const_spec(cb.shape)]
    return pl.pallas_call(
        functools.partial(_odd_in_body, tm=tm),
        grid=(n_rows // tm,),
        in_specs=in_specs, out_specs=(rows(LRU_WIDTH), rows(LRU_WIDTH)),
        out_shape=(jax.ShapeDtypeStruct((n_rows, LRU_WIDTH), F32),
                   jax.ShapeDtypeStruct((n_rows, LRU_WIDTH), F32)),
        scratch_shapes=[pltpu.VMEM((tm + 2 * HALO, D_MODEL), BF16),
                        pltpu.VMEM((LRU_WIDTH // LANES, tm + 2 * HALO, LANES), F32)],
        compiler_params=_params(), name="odd_in",
    )(x, x, x, g, w, cw, cb)


def _lru_body(*refs, reverse, combine, final):
    if combine:
        if final:
            (xc_ref, w_ref, ba_ref, bx_ref, lam_ref, hb_ref, sg_ref, x_ref, wo_ref, fw_ref,
             o_ref, a_scr, u_scr, carry, n_scr, y_scr) = refs
        else:
            (xc_ref, w_ref, ba_ref, bx_ref, lam_ref, hb_ref, sg_ref, x_ref, wo_ref,
             o_ref, a_scr, u_scr, carry, n_scr, y_scr) = refs
    else:
        xc_ref, w_ref, ba_ref, bx_ref, lam_ref, o_ref, a_scr, u_scr, carry = refs
    tm, sub, pitch = LRU_TILE, LRU_SUB, LRU_PITCH
    vregs_per_sub = sub // SUBLANES
    grp_per_blk = LRU_BLOCK // LANES
    half_w = LRU_HALF_GROUPS * LANES

    @pl.when(pl.program_id(0) == 0)
    def _():
        carry[...] = jnp.zeros_like(carry)

    half_rate = (-0.5 * LRU_C) * jax.nn.softplus(-lam_ref[...])
    steps = list(range(sub))
    order = list(range(SUBLANES))
    if reverse:
        steps = steps[::-1]
        order = order[::-1]

    for blk in range(LRU_BLOCKS):
        c0 = blk * LRU_BLOCK
        xb = xc_ref[:, c0:c0 + LRU_BLOCK]
        ri = _dot(xb.astype(BF16), w_ref[blk])
        hr = half_rate[:, c0:c0 + LRU_BLOCK]
        log_a = hr + hr * jnp.tanh(0.5 * (ri[:, :LRU_BLOCK] + ba_ref[:, c0:c0 + LRU_BLOCK]))
        hx = 0.5 * xb
        gx = hx + hx * jnp.tanh(0.5 * (ri[:, LRU_BLOCK:] + bx_ref[:, c0:c0 + LRU_BLOCK]))
        a = jnp.exp(log_a)
        th = jnp.tanh(log_a)
        num = -2.0 * th
        root = jnp.where(num > 0.0, num * lax.rsqrt(num), 0.0) * lax.rsqrt(1.0 - th)
        u = root * gx
        for half in range(grp_per_blk):
            g = blk * grp_per_blk + half
            ls = slice(half * LANES, (half + 1) * LANES)
            for k in range(tm // SUBLANES):
                s = k // vregs_per_sub
                jj = (k % vregs_per_sub) * SUBLANES
                dst = pl.ds(jj * SUBLANES + s, SUBLANES, stride=SUBLANES)
                a_scr[g, dst, :] = a[k * SUBLANES:(k + 1) * SUBLANES, ls]
                u_scr[g, dst, :] = u[k * SUBLANES:(k + 1) * SUBLANES, ls]

    def rows_t(j):
        return slice(j * SUBLANES, (j + 1) * SUBLANES)

    for hf in range(LRU_WIDTH // half_w):
        gs = slice(hf * LRU_HALF_GROUPS, (hf + 1) * LRU_HALF_GROUPS)
        h = u_scr[gs, rows_t(steps[0]), :]
        p = a_scr[gs, rows_t(steps[0]), :]
        for j in steps[1:]:
            aj = a_scr[gs, rows_t(j), :]
            h = aj * h + u_scr[gs, rows_t(j), :]
            p = aj * p
            u_scr[gs, rows_t(j), :] = h
            a_scr[gs, rows_t(j), :] = p

        c = carry[gs]
        carry_in = [None] * SUBLANES
        for s in order:
            carry_in[s] = c
            c = p[:, s:s + 1, :] * c + h[:, s:s + 1, :]
        carry[gs] = c
        cmat = jnp.concatenate(carry_in, axis=1)

        for j in range(sub):
            hj = u_scr[gs, rows_t(j), :] + a_scr[gs, rows_t(j), :] * cmat
            for g in range(LRU_HALF_GROUPS):
                cs_ = slice(hf * half_w + g * LANES, hf * half_w + (g + 1) * LANES)
                if combine:
                    n_scr[g, pl.ds(j, SUBLANES, stride=pitch), :] = hj[g] + hb_ref[rows_t(j), cs_]
                else:
                    o_ref[rows_t(j), cs_] = hj[g]

        if combine:
            for s in range(SUBLANES):
                rs = slice(s * sub, (s + 1) * sub)
                for g in range(LRU_HALF_GROUPS):
                    cs_ = slice(hf * half_w + g * LANES, hf * half_w + (g + 1) * LANES)
                    y_scr[rs, cs_] = (n_scr[g, s * pitch:s * pitch + sub, :] * sg_ref[rs, cs_]).astype(BF16)

    if combine:
        out = x_ref[...] + _dot(y_scr[...], wo_ref[...])
        if final:
            out = _rms(out, fw_ref[...])
        o_ref[...] = out


def _lru(xc, w_gates, b_a, b_x, lam, reverse, combine_args=None, final_w=None):
    n_rows = xc.shape[0]
    tm = LRU_TILE
    n = n_rows // tm
    combine = combine_args is not None
    final = final_w is not None
    rows = functools.partial(_row_spec, tm, rev_n=n if reverse else None)
    in_specs = [rows(LRU_WIDTH), _const_spec(w_gates.shape), _const_spec((1, LRU_WIDTH)),
                _const_spec((1, LRU_WIDTH)), _const_spec((1, LRU_WIDTH))]
    args = [xc, w_gates, b_a, b_x, lam]
    n_grp = LRU_WIDTH // LANES
    scratch = [pltpu.VMEM((n_grp, tm, LANES), F32),
               pltpu.VMEM((n_grp, tm, LANES), F32),
               pltpu.VMEM((n_grp, 1, LANES), F32)]
    if combine:
        h_bwd, sg, x, w_out = combine_args
        in_specs += [rows(LRU_WIDTH), rows(LRU_WIDTH), rows(D_MODEL), _const_spec(w_out.shape)]
        args += [h_bwd, sg, x, w_out]
        if final:
            in_specs.append(_const_spec((1, D_MODEL)))
            args.append(final_w)
        scratch.append(pltpu.VMEM((LRU_HALF_GROUPS, SUBLANES * LRU_PITCH, LANES), F32))
        scratch.append(pltpu.VMEM((tm, LRU_WIDTH), BF16))
        out_w = D_MODEL
    else:
        out_w = LRU_WIDTH
    return pl.pallas_call(
        functools.partial(_lru_body, reverse=reverse, combine=combine, final=final),
        grid=(n,),
        in_specs=in_specs, out_specs=rows(out_w),
        out_shape=jax.ShapeDtypeStruct((n_rows, out_w), F32),
        scratch_shapes=scratch,
        compiler_params=_params(), name="lru_fwd" if combine else "lru_bwd",
    )(*args)


def _rope_tables(n_rows):
    half = ROPE_DIM // 2
    inv = ROPE_THETA ** (-jnp.arange(0, ROPE_DIM, 2, dtype=F32) / ROPE_DIM)
    d = jnp.arange(LANES) % ATTN_HEAD_DIM
    lo = (d < half)[None, :]
    hi = ((d >= half) & (d < ROPE_DIM))[None, :]
    inv_lane = jnp.where(lo | hi, jnp.tile(inv, LANES // half)[None, :], 0.0)
    ang = jnp.arange(n_rows, dtype=F32)[:, None] * inv_lane
    cos_t = jnp.cos(ang)
    sin = jnp.sin(ang)
    return cos_t, jnp.where(lo, -sin, 0.0), jnp.where(hi, sin, 0.0)


def _pad_lanes(v):
    return jnp.pad(v.astype(F32), (0, LANES - v.shape[0]))[None, :]


def _even_layer(x, g_norm, w_in, conv_w, conv_b, dt_bias, a_log, d_skip, ssd_norm_w, sink, w_out, rope):
    n_dt = 2 * SSD_HEADS
    dt0 = SSD_INNER + SSD_CONV_CH
    w_packed = jnp.concatenate(
        [w_in[:, :dt0], w_in[:, dt0 + n_dt:], w_in[:, dt0:dt0 + n_dt],
         jnp.zeros((D_MODEL, LANES - n_dt), w_in.dtype)], axis=1).astype(BF16)
    dtb = _pad_lanes(dt_bias.reshape(-1))
    z, xc, bm, cm, dt, q, kz, vz, sg = _even_in(
        x, g_norm[None, :], w_packed, conv_w, conv_b[None, :], dtb, *rope)

    lane = jnp.arange(LANES)[:, None]
    head = (jnp.arange(SSD_INNER) // SSD_HEAD_DIM)[None, :]
    alog_row = _pad_lanes(a_log.reshape(-1))
    y_bwd = _ssd(xc, bm, cm, dt, alog_row, (lane == head + SSD_HEADS).astype(BF16), reverse=True)
    dskip_x = jnp.repeat(d_skip.astype(F32), SSD_HEAD_DIM)[None, :]
    y_ssd = _ssd(xc, bm, cm, dt, alog_row, (lane == head).astype(BF16), reverse=False,
                 combine_args=(z, y_bwd, dskip_x, ssd_norm_w[None, :]))

    return _attn_out(sink.astype(F32), q, kz, vz, sg, y_ssd, x, w_out.astype(BF16))


def _odd_layer(x, g_norm, w_in, conv_w, conv_b, w_a, b_a, w_x, b_x, lam, w_out, final_w):
    xc, sg = _odd_in(x, g_norm[None, :], w_in.astype(BF16), conv_w, conv_b[None, :])
    w_gates = jnp.concatenate([w_a, w_x], axis=-1).astype(BF16)
    h_bwd = _lru(xc, w_gates[1], b_a[1][None, :], b_x[1][None, :], lam[1][None, :], reverse=True)
    return _lru(xc, w_gates[0], b_a[0][None, :], b_x[0][None, :], lam[0][None, :], reverse=False,
                combine_args=(h_bwd, sg, x, w_out.astype(BF16)),
                final_w=None if final_w is None else final_w[None, :])


def kernel(x, norm_w, final_norm_w, ev_w_in, ev_conv_w, ev_conv_b, ev_dt_bias, ev_a_log, ev_d_skip, ev_ssd_norm_w, ev_sink, ev_w_out, od_w_in, od_conv_w, od_conv_b, od_w_a, od_b_a, od_w_x, od_b_x, od_lambda, od_w_out):
    batch, n_rows, _ = x.shape
    depth = norm_w.shape[0]
    assert batch == 1 and depth % 2 == 0
    rope = _rope_tables(n_rows)
    h = x[0]
    for layer in range(depth):
        j = layer // 2
        if layer % 2 == 0:
            h = _even_layer(h, norm_w[layer], ev_w_in[j], ev_conv_w[j], ev_conv_b[j], ev_dt_bias[j],
                            ev_a_log[j], ev_d_skip[j], ev_ssd_norm_w[j], ev_sink[j], ev_w_out[j], rope)
        else:
            h = _odd_layer(h, norm_w[layer], od_w_in[j], od_conv_w[j], od_conv_b[j], od_w_a[j],
                           od_b_a[j], od_w_x[j], od_b_x[j], od_lambda[j], od_w_out[j],
                           final_norm_w if layer == depth - 1 else None)
    return h[None]
```
